```python
import jax, jax.numpy as jnp
from jax import lax
import numpy as np

D_MODEL = 2048
BATCH = 4
SEQ = 2048
DEPTH = 4
DEC_BATCH = 128
DEC_SEQ = 1
PAST_LEN = 16384
PAGE_SIZE = 128

N_MIXERS = 4
D_MIX = D_MODEL
D_GRP = D_MIX // N_MIXERS
D_IN = 8 * D_GRP
CONV_A = 3
POOL_WINDOWS = (2, 4, 8, 16)
N_POOL = len(POOL_WINDOWS)
D_POOL = D_GRP // N_POOL
POOL_HIST = max(POOL_WINDOWS) - 1
CHUNK = 128
N_HEADS_C = 4
D_HEAD_C = D_GRP // N_HEADS_C
CONV_D = 31
N_MEM = 256
N_HEADS_X = 4
D_HEAD_X = D_MODEL // N_HEADS_X
D_FF = 5632
N_EXPERTS = 8
TOP_K = 2
N_DENSE = (DEPTH + 1) // 2
N_MOE = DEPTH // 2
ALPHA = (2 * DEPTH) ** 0.25
BETA = (8 * DEPTH) ** -0.25
LN_EPS = 1e-5

kernel_name = "hybrid_conv_pool_sgu_conformer_decoder_step"


def layer_norm(x, g, b):
    xf = x.astype(jnp.float32)
    mu = jnp.mean(xf, axis=-1, keepdims=True)
    var = jnp.mean(jnp.square(xf - mu), axis=-1, keepdims=True)
    return ((xf - mu) * lax.rsqrt(var + LN_EPS)).astype(x.dtype) * g + b


def causal_dwconv(ext, w):
    c = ext.shape[-1]
    return lax.conv_general_dilated(
        ext, w[:, None, :].astype(ext.dtype), window_strides=(1,), padding="VALID",
        dimension_numbers=("NWC", "WIO", "NWC"), feature_group_count=c)


def multiscale_pool(ext, pos0):
    L = ext.shape[1] - POOL_HIST
    cs = jnp.pad(jnp.cumsum(ext.astype(jnp.float32), axis=1), ((0, 0), (1, 0), (0, 0)))
    cur = ext[:, POOL_HIST:].astype(jnp.float32)
    pos = pos0 + jnp.arange(L)
    outs = []
    for g, w in enumerate(POOL_WINDOWS):
        sl = slice(g * D_POOL, (g + 1) * D_POOL)
        s = cs[:, POOL_HIST + 1:POOL_HIST + 1 + L, sl] - cs[:, POOL_HIST + 1 - w:POOL_HIST + 1 - w + L, sl]
        cnt = jnp.minimum(pos + 1, w).astype(jnp.float32)[None, :, None]
        outs.append(s / cnt - cur[..., sl])
    return jnp.stack(outs, axis=2).astype(ext.dtype)


def chunk_mix(v, w_s, b_s):
    bsz, L = v.shape[0], v.shape[1]
    c = min(L, CHUNK)
    n = -(-L // c)
    vp = jnp.pad(v, ((0, 0), (0, n * c - L), (0, 0), (0, 0)))
    vp = vp.reshape(bsz, n, c, N_HEADS_C, D_HEAD_C)
    w = jnp.tril(w_s[:, :c, :c])
    out = jnp.einsum("hts,bnshd->bnthd", w, vp) + b_s[:, :c].T[None, None, :, :, None]
    return out.reshape(bsz, n * c, N_HEADS_C, D_HEAD_C)[:, :L]


def parallel_mixer(x, hist_a, hist_b, hist_d, pos0, w_in, conv_a, pool_w, pool_scale,
                   sg_ln_g, sg_ln_b, sg_w, sg_b, conv_d, cd_ln_g, cd_ln_b, w_out):
    bsz, L, _ = x.shape
    h = x @ w_in
    a_b, a_c, a_h, b_v, c_u, c_v, d_a, d_g = jnp.split(h, 8, axis=-1)
    ext_a = jnp.concatenate([hist_a, a_c * a_h], axis=1)
    y_a = a_b * causal_dwconv(ext_a, conv_a)
    ext_b = jnp.concatenate([hist_b, b_v], axis=1)
    pooled = multiscale_pool(ext_b, pos0)
    y_b = jnp.einsum("blgc,gcd->blgd", pooled, pool_w).reshape(bsz, L, D_GRP) * pool_scale
    v_n = layer_norm(c_v, sg_ln_g, sg_ln_b)
    mixed = chunk_mix(v_n.reshape(bsz, L, N_HEADS_C, D_HEAD_C), sg_w, sg_b)
    y_c = c_u * mixed.reshape(bsz, L, D_GRP)
    ext_d = jnp.concatenate([hist_d, d_a * jax.nn.sigmoid(d_g)], axis=1)
    y_d = jax.nn.silu(layer_norm(causal_dwconv(ext_d, conv_d), cd_ln_g, cd_ln_b))
    y = jnp.concatenate([y_a, y_b, y_c, y_d], axis=-1) @ w_out
    new_state = (ext_a[:, -(CONV_A - 1):], ext_b[:, -POOL_HIST:], ext_d[:, -(CONV_D - 1):], v_n)
    return y, new_state


def cross_attention(x, mem_k, mem_v, w_q, w_o):
    bsz, L, _ = x.shape
    q = (x @ w_q).reshape(bsz, L, N_HEADS_X, D_HEAD_X)
    s = jnp.einsum("blhd,bmhd->bhlm", q, mem_k, preferred_element_type=jnp.float32) * (D_HEAD_X ** -0.5)
    p = jax.nn.softmax(s, axis=-1).astype(x.dtype)
    o = jnp.einsum("bhlm,bmhd->blhd", p, mem_v).reshape(bsz, L, D_MODEL)
    return o @ w_o


def swiglu(x, w_gate, w_up, w_down):
    return (jax.nn.silu(x @ w_gate) * (x @ w_up)) @ w_down


def moe_swiglu(x, router, w_gate, w_up, w_down):
    logits = jnp.einsum("bld,de->ble", x, router, preferred_element_type=jnp.float32)
    top_v, top_i = lax.top_k(logits, TOP_K)
    gates = jax.nn.softmax(top_v, axis=-1)
    comb = jnp.sum(gates[..., None] * jax.nn.one_hot(top_i, N_EXPERTS, dtype=jnp.float32), axis=-2)
    comb = comb.astype(x.dtype)
    y = jnp.zeros_like(x)
    for e in range(N_EXPERTS):
        y = y + comb[..., e:e + 1] * swiglu(x, w_gate[e], w_up[e], w_down[e])
    return y


def decoder_layer(x, mem_k, mem_v, hist_a, hist_b, hist_d, pos0, mix_w, w_q, w_o, ln_g, ln_b, ffn, ffn_w):
    y, new_state = parallel_mixer(x, hist_a, hist_b, hist_d, pos0, *mix_w)
    x = layer_norm(ALPHA * x + y, ln_g[0], ln_b[0])
    x = layer_norm(ALPHA * x + cross_attention(x, mem_k, mem_v, w_q, w_o), ln_g[1], ln_b[1])
    x = layer_norm(ALPHA * x + ffn(x, *ffn_w), ln_g[2], ln_b[2])
    return x, new_state


def setup_inputs(seed: int = 0) -> dict:
    key = jax.random.key(seed)
    ks = list(jax.random.split(key, 40))
    ctr = [0]

    def nrm(shape, scale=1.0):
        k = ks[ctr[0]]
        ctr[0] += 1
        return jax.random.normal(k, shape, jnp.float32) * scale

    def gain(shape):
        return 1.0 + 0.02 * nrm(shape)

    d = D_MODEL
    return {
        "x_prompt": nrm((BATCH, SEQ, d)),
        "x_sample": nrm((DEC_BATCH, DEC_SEQ, d)),
        "state_a": nrm((DEPTH, DEC_BATCH, CONV_A - 1, D_GRP)),
        "state_b": nrm((DEPTH, DEC_BATCH, POOL_HIST, D_GRP)),
        "state_d": nrm((DEPTH, DEC_BATCH, CONV_D - 1, D_GRP), 0.5),
        "cache_mem_k": nrm((DEPTH, DEC_BATCH, N_MEM, N_HEADS_X, D_HEAD_X)),
        "cache_mem_v": nrm((DEPTH, DEC_BATCH, N_MEM, N_HEADS_X, D_HEAD_X)),
        "mem_prompt": nrm((BATCH, N_MEM, d)),
        "w_in": nrm((DEPTH, d, D_IN), d ** -0.5),
        "conv_a": nrm((DEPTH, CONV_A, D_GRP), CONV_A ** -0.5),
        "pool_w": nrm((DEPTH, N_POOL, D_POOL, D_POOL), D_POOL ** -0.5),
        "pool_scale": gain((DEPTH, D_GRP)),
        "sg_ln_g": gain((DEPTH, D_GRP)),
        "sg_ln_b": nrm((DEPTH, D_GRP), 0.02),
        "sg_w": nrm((DEPTH, N_HEADS_C, CHUNK, CHUNK), CHUNK ** -0.5),
        "sg_b": gain((DEPTH, N_HEADS_C, CHUNK)),
        "conv_d": nrm((DEPTH, CONV_D, D_GRP), CONV_D ** -0.5),
        "cd_ln_g": gain((DEPTH, D_GRP)),
        "cd_ln_b": nrm((DEPTH, D_GRP), 0.02),
        "w_out": nrm((DEPTH, D_MIX, d), BETA * D_MIX ** -0.5),
        "w_q": nrm((DEPTH, d, d), d ** -0.5),
        "w_k": nrm((DEPTH, d, d), d ** -0.5),
        "w_v": nrm((DEPTH, d, d), d ** -0.5),
        "w_o": nrm((DEPTH, d, d), BETA * d ** -0.5),
        "ln_g": gain((DEPTH, 3, d)),
        "ln_b": nrm((DEPTH, 3, d), 0.02),
        "dense_w_gate": nrm((N_DENSE, d, D_FF), d ** -0.5),
        "dense_w_up": nrm((N_DENSE, d, D_FF), d ** -0.5),
        "dense_w_down": nrm((N_DENSE, D_FF, d), BETA * D_FF ** -0.5),
        "moe_router": nrm((N_MOE, d, N_EXPERTS), d ** -0.5),
        "moe_w_gate": nrm((N_MOE, N_EXPERTS, d, D_FF), d ** -0.5),
        "moe_w_up": nrm((N_MOE, N_EXPERTS, d, D_FF), d ** -0.5),
        "moe_w_down": nrm((N_MOE, N_EXPERTS, D_FF, d), BETA * D_FF ** -0.5),
    }


def reference(x_prompt, x_sample, state_a, state_b, state_d, cache_mem_k, cache_mem_v, mem_prompt,
              w_in, conv_a, pool_w, pool_scale, sg_ln_g, sg_ln_b, sg_w, sg_b, conv_d, cd_ln_g, cd_ln_b,
              w_out, w_q, w_k, w_v, w_o, ln_g, ln_b, dense_w_gate, dense_w_up, dense_w_down,
              moe_router, moe_w_gate, moe_w_up, moe_w_down):
    bp = x_prompt.shape[0]
    dt = x_prompt.dtype
    xp, xs = x_prompt, x_sample
    zero_a = jnp.zeros((bp, CONV_A - 1, D_GRP), dt)
    zero_b = jnp.zeros((bp, POOL_HIST, D_GRP), dt)
    zero_d = jnp.zeros((bp, CONV_D - 1, D_GRP), dt)
    sa_p, sb_p, sd_p, mk_p_all, mv_p_all = [], [], [], [], []
    sa_s, sb_s, sd_s, sc_s = [], [], [], []
    for l in range(DEPTH):
        mix_w = (w_in[l], conv_a[l], pool_w[l], pool_scale[l], sg_ln_g[l], sg_ln_b[l], sg_w[l], sg_b[l],
                 conv_d[l], cd_ln_g[l], cd_ln_b[l], w_out[l])
        j = l // 2
        if l % 2 == 0:
            ffn, ffn_w = swiglu, (dense_w_gate[j], dense_w_up[j], dense_w_down[j])
        else:
            ffn, ffn_w = moe_swiglu, (moe_router[j], moe_w_gate[j], moe_w_up[j], moe_w_down[j])
        mk_p = (mem_prompt @ w_k[l]).reshape(bp, N_MEM, N_HEADS_X, D_HEAD_X)
        mv_p = (mem_prompt @ w_v[l]).reshape(bp, N_MEM, N_HEADS_X, D_HEAD_X)
        xp, (ha, hb, hd, _) = decoder_layer(xp, mk_p, mv_p, zero_a, zero_b, zero_d, 0, mix_w,
                                            w_q[l], w_o[l], ln_g[l], ln_b[l], ffn, ffn_w)
        xs, (ta, tb, td, tc) = decoder_layer(xs, cache_mem_k[l], cache_mem_v[l], state_a[l], state_b[l],
                                             state_d[l], PAST_LEN, mix_w, w_q[l], w_o[l], ln_g[l], ln_b[l],
                                             ffn, ffn_w)
        sa_p.append(ha); sb_p.append(hb); sd_p.append(hd); mk_p_all.append(mk_p); mv_p_all.append(mv_p)
        sa_s.append(ta); sb_s.append(tb); sd_s.append(td); sc_s.append(tc)
    return (xp, xs, jnp.stack(sa_p), jnp.stack(sb_p), jnp.stack(sd_p), jnp.stack(mk_p_all), jnp.stack(mv_p_all),
            jnp.stack(sa_s), jnp.stack(sb_s), jnp.stack(sd_s), jnp.stack(sc_s))
```

```python
import functools

import jax
import jax.numpy as jnp
from jax import lax
from jax.experimental import pallas as pl
from jax.experimental.pallas import tpu as pltpu

F32 = jnp.float32
BF16 = jnp.bfloat16

D_MODEL = 2048
BATCH = 4
SEQ = 2048
DEPTH = 4
DEC_BATCH = 128
PAST_LEN = 16384
D_GRP = 512
D_IN = 8 * D_GRP
CONV_A = 3
POOL_WINDOWS = (2, 4, 8, 16)
D_POOL = 128
POOL_HIST = 15
CHUNK = 128
N_HEADS_C = 4
CONV_D = 31
N_MEM = 256
N_HEADS_X = 4
D_HEAD_X = 512
D_FF = 5632
N_EXPERTS = 8
ALPHA = (2 * DEPTH) ** 0.25
LN_EPS = 1e-5
ATTN_SCALE = D_HEAD_X ** -0.5

N_PROMPT = BATCH * SEQ
N_TOK = N_PROMPT + DEC_BATCH

VMEM_LIMIT_BYTES = 52 * 1024 * 1024

TM = 640
TM_LN = 320
TN = 1024
TF = 512
TK_DOWN = 512
TN_DOWN = 512
T_MIX = 256
ROWS_D = 32
HALO_A, HALO_B, HALO_D = 8, 16, 32
S_MIX = 32
TQ = 512
BS_ATT = 4
TM_COMB = 128


def _params(n_axes):
    return pltpu.CompilerParams(dimension_semantics=("arbitrary",) * n_axes,
                                vmem_limit_bytes=VMEM_LIMIT_BYTES)


def _layer_norm(x, g, b):
    mu = jnp.mean(x, axis=-1, keepdims=True)
    xc = x - mu
    var = jnp.mean(xc * xc, axis=-1, keepdims=True)
    return xc * lax.rsqrt(var + LN_EPS) * g + b


def _sigmoid(x):
    return 1.0 / (1.0 + jnp.exp(-x))


def _mm_body(x_ref, w_ref, o_ref, wb_ref):
    @pl.when(pl.program_id(1) == 0)
    def _():
        wb_ref[...] = w_ref[...].astype(BF16)

    o_ref[...] = jnp.dot(x_ref[...], wb_ref[...], preferred_element_type=F32).astype(o_ref.dtype)


def _matmul(x, w, out_dtype, tm, tn, name):
    m, k = x.shape
    n = w.shape[1]
    return pl.pallas_call(
        _mm_body,
        out_shape=jax.ShapeDtypeStruct((m, n), out_dtype),
        grid=(n // tn, m // tm),
        in_specs=[pl.BlockSpec((tm, k), lambda j, i: (i, 0)),
                  pl.BlockSpec((k, tn), lambda j, i: (0, j))],
        out_specs=pl.BlockSpec((tm, tn), lambda j, i: (i, j)),
        scratch_shapes=[pltpu.VMEM((k, tn), BF16)],
        compiler_params=_params(2),
        name=name,
    )(x, w)


def _residual_ln_store(acc, res_ref, g_ref, b_ref, o_ref, ob_ref):
    y = _layer_norm(ALPHA * res_ref[...] + acc, g_ref[...], b_ref[...])
    o_ref[...] = y
    ob_ref[...] = y.astype(BF16)


def _mm_ln_resident_body(a_ref, w_ref, res_ref, g_ref, b_ref, o_ref, ob_ref, wb_ref):
    @pl.when(pl.program_id(0) == 0)
    def _():
        wb_ref[...] = w_ref[...].astype(BF16)

    acc = jnp.dot(a_ref[...], wb_ref[...], preferred_element_type=F32)
    _residual_ln_store(acc, res_ref, g_ref, b_ref, o_ref, ob_ref)


def _mm_ln_resident(a, w, res, g, b, name):
    m, k = a.shape
    n = w.shape[1]
    tm = TM_LN
    row = lambda i: (i, 0)
    fixed = lambda i: (0, 0)
    return pl.pallas_call(
        _mm_ln_resident_body,
        out_shape=(jax.ShapeDtypeStruct((m, n), F32), jax.ShapeDtypeStruct((m, n), BF16)),
        grid=(m // tm,),
        in_specs=[pl.BlockSpec((tm, k), row),
                  pl.BlockSpec((k, n), fixed, pipeline_mode=pl.Buffered(1)),
                  pl.BlockSpec((tm, n), row),
                  pl.BlockSpec((1, n), fixed),
                  pl.BlockSpec((1, n), fixed)],
        out_specs=(pl.BlockSpec((tm, n), row), pl.BlockSpec((tm, n), row)),
        scratch_shapes=[pltpu.VMEM((k, n), BF16)],
        compiler_params=_params(1),
        name=name,
    )(a, w, res, g, b)


def _mm_ln_ktiled_body(a_ref, w_ref, res_ref, g_ref, b_ref, o_ref, ob_ref, acc_ref, *, nk):
    k = pl.program_id(1)
    part = jnp.dot(a_ref[...], w_ref[...].astype(BF16), preferred_element_type=F32)

    @pl.when(k == 0)
    def _():
        acc_ref[...] = part

    @pl.when(k > 0)
    def _():
        acc_ref[...] += part

    @pl.when(k == nk - 1)
    def _():
        _residual_ln_store(acc_ref[...], res_ref, g_ref, b_ref, o_ref, ob_ref)


def _mm_ln_ktiled(a, w, res, g, b, name):
    m, k = a.shape
    n = w.shape[1]
    tm, tk = TM, TK_DOWN
    nk = k // tk
    row = lambda i, kk: (i, 0)
    fixed = lambda i, kk: (0, 0)
    return pl.pallas_call(
        functools.partial(_mm_ln_ktiled_body, nk=nk),
        out_shape=(jax.ShapeDtypeStruct((m, n), F32), jax.ShapeDtypeStruct((m, n), BF16)),
        grid=(m // tm, nk),
        in_specs=[pl.BlockSpec((tm, tk), lambda i, kk: (i, kk)),
                  pl.BlockSpec((tk, n), lambda i, kk: (kk, 0)),
                  pl.BlockSpec((tm, n), row),
                  pl.BlockSpec((1, n), fixed),
                  pl.BlockSpec((1, n), fixed)],
        out_specs=(pl.BlockSpec((tm, n), row), pl.BlockSpec((tm, n), row)),
        scratch_shapes=[pltpu.VMEM((tm, n), F32)],
        compiler_params=_params(2),
        name=name,
    )(a, w, res, g, b)


def _mix_prompt_body(h_ref, ca_ref, pw_ref, ps_ref, lg_ref, lb_ref, sw_ref, sbt_ref, cd_ref, dg_ref, db_ref,
                     y_ref, sa_ref, sb_ref, sd_ref, ea_ref, eb_ref, ed_ref, vn_ref, *, n_steps):
    t = T_MIX
    g = D_GRP
    s = pl.program_id(1)

    @pl.when(s == 0)
    def _():
        ea_ref[0:HALO_A, :] = jnp.zeros((HALO_A, g), F32)
        eb_ref[0:HALO_B, :] = jnp.zeros((HALO_B, g), F32)
        ed_ref[0:HALO_D, :] = jnp.zeros((HALO_D, g), F32)

    e = h_ref[:, g:2 * g] * h_ref[:, 2 * g:3 * g]
    ea_ref[HALO_A:HALO_A + t, :] = e
    conv_a = (ca_ref[2:3, :] * e + ca_ref[1:2, :] * ea_ref[HALO_A - 1:HALO_A - 1 + t, :]
              + ca_ref[0:1, :] * ea_ref[HALO_A - 2:HALO_A - 2 + t, :])
    y_ref[:, 0:g] = (h_ref[:, 0:g] * conv_a).astype(BF16)

    eb_ref[HALO_B:HALO_B + t, :] = h_ref[:, 3 * g:4 * g]
    pos = s * t + lax.broadcasted_iota(jnp.int32, (t, 1), 0)
    for gi, w in enumerate(POOL_WINDOWS):
        lo = gi * D_POOL
        cur = h_ref[:, 3 * g + lo:3 * g + lo + D_POOL]
        win = cur
        for k in range(1, w):
            win = win + eb_ref[HALO_B - k:HALO_B - k + t, lo:lo + D_POOL]
        cnt = jnp.minimum(pos + 1, w).astype(F32)
        pooled = win / cnt - cur
        yb = jnp.dot(pooled.astype(BF16), pw_ref[gi].astype(BF16), preferred_element_type=F32)
        y_ref[:, g + lo:g + lo + D_POOL] = (yb * ps_ref[:, lo:lo + D_POOL]).astype(BF16)

    vn_ref[...] = _layer_norm(h_ref[:, 5 * g:6 * g], lg_ref[...], lb_ref[...])
    n_chunks = t // CHUNK
    rows = lax.broadcasted_iota(jnp.int32, (CHUNK, CHUNK), 0)
    cols = lax.broadcasted_iota(jnp.int32, (CHUNK, CHUNK), 1)
    for hh in range(N_HEADS_C):
        lo = hh * CHUNK
        w_tril = jnp.where(rows >= cols, sw_ref[hh], 0.0).astype(BF16)
        rhs = jnp.concatenate([vn_ref[c * CHUNK:(c + 1) * CHUNK, lo:lo + CHUNK] for c in range(n_chunks)], axis=1)
        mixed = jnp.dot(w_tril, rhs.astype(BF16), preferred_element_type=F32)
        bias = sbt_ref[:, hh:hh + 1]
        for c in range(n_chunks):
            u = h_ref[c * CHUNK:(c + 1) * CHUNK, 4 * g + lo:4 * g + lo + CHUNK]
            y_ref[c * CHUNK:(c + 1) * CHUNK, 2 * g + lo:2 * g + lo + CHUNK] = (
                u * (mixed[:, c * CHUNK:(c + 1) * CHUNK] + bias)).astype(BF16)

    ed_ref[HALO_D:HALO_D + t, :] = h_ref[:, 6 * g:7 * g] * _sigmoid(h_ref[:, 7 * g:8 * g])
    for rb in range(t // ROWS_D):
        r0 = HALO_D + rb * ROWS_D
        acc = cd_ref[CONV_D - 1:CONV_D, :] * ed_ref[r0:r0 + ROWS_D, :]
        for k in range(CONV_D - 1):
            off = r0 - (CONV_D - 1) + k
            acc = acc + cd_ref[k:k + 1, :] * ed_ref[off:off + ROWS_D, :]
        z = _layer_norm(acc, dg_ref[...], db_ref[...])
        y_ref[rb * ROWS_D:(rb + 1) * ROWS_D, 3 * g:4 * g] = (z * _sigmoid(z)).astype(BF16)

    @pl.when(s == n_steps - 1)
    def _():
        sa_ref[0] = ea_ref[HALO_A + t - (CONV_A - 1):HALO_A + t, :]
        sb_ref[0] = eb_ref[HALO_B + t - POOL_HIST:HALO_B + t, :]
        sd_ref[0] = ed_ref[HALO_D + t - (CONV_D - 1):HALO_D + t, :]

    ea_ref[0:HALO_A, :] = ea_ref[t:t + HALO_A, :]
    eb_ref[0:HALO_B, :] = eb_ref[t:t + HALO_B, :]
    ed_ref[0:HALO_D, :] = ed_ref[t:t + HALO_D, :]


def _mix_prompt(h, ca, pw, ps, lg, lb, sw, sbt, cd, dg, db):
    n_steps = SEQ // T_MIX
    g = D_GRP
    c2 = lambda b, s: (0, 0)
    c3 = lambda b, s: (0, 0, 0)
    st = lambda b, s: (b, 0, 0)
    return pl.pallas_call(
        functools.partial(_mix_prompt_body, n_steps=n_steps),
        out_shape=(jax.ShapeDtypeStruct((N_PROMPT, D_MODEL), BF16),
                   jax.ShapeDtypeStruct((BATCH, CONV_A - 1, g), F32),
                   jax.ShapeDtypeStruct((BATCH, POOL_HIST, g), F32),
                   jax.ShapeDtypeStruct((BATCH, CONV_D - 1, g), F32)),
        grid=(BATCH, n_steps),
        in_specs=[pl.BlockSpec((T_MIX, D_IN), lambda b, s: (b * n_steps + s, 0)),
                  pl.BlockSpec((CONV_A, g), c2),
                  pl.BlockSpec((len(POOL_WINDOWS), D_POOL, D_POOL), c3),
                  pl.BlockSpec((1, g), c2),
                  pl.BlockSpec((1, g), c2),
                  pl.BlockSpec((1, g), c2),
                  pl.BlockSpec((N_HEADS_C, CHUNK, CHUNK), c3),
                  pl.BlockSpec((CHUNK, N_HEADS_C), c2),
                  pl.BlockSpec((CONV_D, g), c2),
                  pl.BlockSpec((1, g), c2),
                  pl.BlockSpec((1, g), c2)],
        out_specs=(pl.BlockSpec((T_MIX, D_MODEL), lambda b, s: (b * n_steps + s, 0)),
                   pl.BlockSpec((1, CONV_A - 1, g), st),
                   pl.BlockSpec((1, POOL_HIST, g), st),
                   pl.BlockSpec((1, CONV_D - 1, g), st)),
        scratch_shapes=[pltpu.VMEM((HALO_A + T_MIX, g), F32),
                        pltpu.VMEM((HALO_B + T_MIX, g), F32),
                        pltpu.VMEM((HALO_D + T_MIX, g), F32),
                        pltpu.VMEM((T_MIX, g), F32)],
        compiler_params=_params(2),
        name="mix_prompt",
    )(h, ca, pw, ps, lg, lb, sw, sbt, cd, dg, db)


def _mix_sample_body(h_ref, sa_ref, sb_ref, sd_ref, ca_ref, pw_ref, ps_ref, lg_ref, lb_ref, w0_ref, b0_ref,
                     cd_ref, dg_ref, db_ref, y_ref, na_ref, nb_ref, nd_ref, vn_ref):
    g = D_GRP

    e = h_ref[:, g:2 * g] * h_ref[:, 2 * g:3 * g]
    conv_a = ca_ref[2:3, :] * e + ca_ref[1:2, :] * sa_ref[:, g:2 * g] + ca_ref[0:1, :] * sa_ref[:, 0:g]
    y_ref[:, 0:g] = (h_ref[:, 0:g] * conv_a).astype(BF16)
    na_ref[:, 0:g] = sa_ref[:, g:2 * g]
    na_ref[:, g:2 * g] = e

    for gi, w in enumerate(POOL_WINDOWS):
        lo = gi * D_POOL
        cur = h_ref[:, 3 * g + lo:3 * g + lo + D_POOL]
        win = cur
        for k in range(1, w):
            col = (POOL_HIST - k) * g + lo
            win = win + sb_ref[:, col:col + D_POOL]
        cnt = float(min(PAST_LEN + 1, w))
        pooled = win / cnt - cur
        yb = jnp.dot(pooled.astype(BF16), pw_ref[gi].astype(BF16), preferred_element_type=F32)
        y_ref[:, g + lo:g + lo + D_POOL] = (yb * ps_ref[:, lo:lo + D_POOL]).astype(BF16)
    nb_ref[:, 0:(POOL_HIST - 1) * g] = sb_ref[:, g:POOL_HIST * g]
    nb_ref[:, (POOL_HIST - 1) * g:POOL_HIST * g] = h_ref[:, 3 * g:4 * g]

    vn = _layer_norm(h_ref[:, 5 * g:6 * g], lg_ref[...], lb_ref[...])
    vn_ref[...] = vn
    y_ref[:, 2 * g:3 * g] = (h_ref[:, 4 * g:5 * g] * (w0_ref[...] * vn + b0_ref[...])).astype(BF16)

    glu = h_ref[:, 6 * g:7 * g] * _sigmoid(h_ref[:, 7 * g:8 * g])
    acc = cd_ref[CONV_D - 1:CONV_D, :] * glu
    for k in range(CONV_D - 1):
        acc = acc + cd_ref[k:k + 1, :] * sd_ref[:, k * g:(k + 1) * g]
    z = _layer_norm(acc, dg_ref[...], db_ref[...])
    y_ref[:, 3 * g:4 * g] = (z * _sigmoid(z)).astype(BF16)
    nd_ref[:, 0:(CONV_D - 2) * g] = sd_ref[:, g:(CONV_D - 1) * g]
    nd_ref[:, (CONV_D - 2) * g:(CONV_D - 1) * g] = glu


def _mix_sample(h, sa, sb, sd, ca, pw, ps, lg, lb, w0, b0, cd, dg, db):
    g = D_GRP
    n_steps = DEC_BATCH // S_MIX
    first = N_PROMPT // S_MIX
    row = lambda i: (i, 0)
    c2 = lambda i: (0, 0)
    c3 = lambda i: (0, 0, 0)
    wa, wb, wd = (CONV_A - 1) * g, POOL_HIST * g, (CONV_D - 1) * g
    return pl.pallas_call(
        _mix_sample_body,
        out_shape=(jax.ShapeDtypeStruct((DEC_BATCH, D_MODEL), BF16),
                   jax.ShapeDtypeStruct((DEC_BATCH, wa), F32),
                   jax.ShapeDtypeStruct((DEC_BATCH, wb), F32),
                   jax.ShapeDtypeStruct((DEC_BATCH, wd), F32),
                   jax.ShapeDtypeStruct((DEC_BATCH, g), F32)),
        grid=(n_steps,),
        in_specs=[pl.BlockSpec((S_MIX, D_IN), lambda i: (first + i, 0)),
                  pl.BlockSpec((S_MIX, wa), row),
                  pl.BlockSpec((S_MIX, wb), row),
                  pl.BlockSpec((S_MIX, wd), row),
                  pl.BlockSpec((CONV_A, g), c2),
                  pl.BlockSpec((len(POOL_WINDOWS), D_POOL, D_POOL), c3),
                  pl.BlockSpec((1, g), c2),
                  pl.BlockSpec((1, g), c2),
                  pl.BlockSpec((1, g), c2),
                  pl.BlockSpec((1, g), c2),
                  pl.BlockSpec((1, g), c2),
                  pl.BlockSpec((CONV_D, g), c2),
                  pl.BlockSpec((1, g), c2),
                  pl.BlockSpec((1, g), c2)],
        out_specs=(pl.BlockSpec((S_MIX, D_MODEL), row),
                   pl.BlockSpec((S_MIX, wa), row),
                   pl.BlockSpec((S_MIX, wb), row),
                   pl.BlockSpec((S_MIX, wd), row),
                   pl.BlockSpec((S_MIX, g), row)),
        compiler_params=_params(1),
        name="mix_sample",
    )(h, sa, sb, sd, ca, pw, ps, lg, lb, w0, b0, cd, dg, db)


def _attn_prompt_body(q_ref, k_ref, v_ref, o_ref):
    for hh in range(N_HEADS_X):
        sl = slice(hh * D_HEAD_X, (hh + 1) * D_HEAD_X)
        kh = k_ref[:, sl].astype(BF16)
        vh = v_ref[:, sl].astype(BF16)
        s = lax.dot_general(q_ref[:, sl], kh, (((1,), (1,)), ((), ())), preferred_element_type=F32) * ATTN_SCALE
        p = jnp.exp(s - jnp.max(s, axis=-1, keepdims=True))
        p = p / jnp.sum(p, axis=-1, keepdims=True)
        o_ref[:, sl] = jnp.dot(p.astype(BF16), vh, preferred_element_type=F32).astype(BF16)


def _attn_prompt(q, mk, mv):
    n_q = SEQ // TQ
    kv = pl.BlockSpec((N_MEM, D_MODEL), lambda b, i: (b, 0))
    qo = pl.BlockSpec((TQ, D_MODEL), lambda b, i: (b * n_q + i, 0))
    return pl.pallas_call(
        _attn_prompt_body,
        out_shape=jax.ShapeDtypeStruct((N_PROMPT, D_MODEL), BF16),
        grid=(BATCH, n_q),
        in_specs=[qo, kv, kv],
        out_specs=qo,
        compiler_params=_params(2),
        name="attn_prompt",
    )(q, mk, mv)


def _attn_sample_body(q_ref, k_ref, v_ref, o_ref):
    for j in range(BS_ATT):
        for hh in range(N_HEADS_X):
            sl = slice(hh * D_HEAD_X, (hh + 1) * D_HEAD_X)
            s = jnp.sum(k_ref[j, :, sl] * q_ref[j, :, sl], axis=-1, keepdims=True) * ATTN_SCALE
            p = jnp.exp(s - jnp.max(s, axis=0, keepdims=True))
            p = p / jnp.sum(p, axis=0, keepdims=True)
            o_ref[j, :, sl] = jnp.sum(p * v_ref[j, :, sl], axis=0, keepdims=True)


def _attn_sample(q, ck, cv):
    blk = lambda i: (i, 0, 0)
    qo = pl.BlockSpec((BS_ATT, 1, D_MODEL), blk)
    kv = pl.BlockSpec((BS_ATT, N_MEM, D_MODEL), blk)
    return pl.pallas_call(
        _attn_sample_body,
        out_shape=jax.ShapeDtypeStruct((DEC_BATCH, 1, D_MODEL), F32),
        grid=(DEC_BATCH // BS_ATT,),
        in_specs=[qo, kv, kv],
        out_specs=qo,
        compiler_params=_params(1),
        name="attn_sample",
    )(q, ck, cv)


def _weights_changed(te_ref, t):
    return jnp.logical_or(t == 0, te_ref[t] != te_ref[jnp.maximum(t - 1, 0)])


def _ffn_up_body(te_ref, tx_ref, x_ref, wg_ref, wu_ref, o_ref, wgb_ref, wub_ref):
    @pl.when(_weights_changed(te_ref, pl.program_id(1)))
    def _():
        wgb_ref[...] = wg_ref[...].astype(BF16)
        wub_ref[...] = wu_ref[...].astype(BF16)

    x = x_ref[...]
    gate = jnp.dot(x, wgb_ref[...], preferred_element_type=F32)
    up = jnp.dot(x, wub_ref[...], preferred_element_type=F32)
    o_ref[...] = (gate * _sigmoid(gate) * up).astype(BF16)


def _ffn_up(x, wg, wu, tile_expert, tile_xblock):
    n_tiles = tile_expert.shape[0]
    k = x.shape[1]
    grid_spec = pltpu.PrefetchScalarGridSpec(
        num_scalar_prefetch=2,
        grid=(D_FF // TF, n_tiles),
        in_specs=[pl.BlockSpec((TM, k), lambda f, t, te, tx: (tx[t], 0)),
                  pl.BlockSpec((None, k, TF), lambda f, t, te, tx: (te[t], 0, f)),
                  pl.BlockSpec((None, k, TF), lambda f, t, te, tx: (te[t], 0, f))],
        out_specs=pl.BlockSpec((TM, TF), lambda f, t, te, tx: (t, f)),
        scratch_shapes=[pltpu.VMEM((k, TF), BF16), pltpu.VMEM((k, TF), BF16)])
    return pl.pallas_call(
        _ffn_up_body,
        out_shape=jax.ShapeDtypeStruct((n_tiles * TM, D_FF), BF16),
        grid_spec=grid_spec,
        compiler_params=_params(2),
        name="ffn_up",
    )(tile_expert, tile_xblock, x, wg, wu)


def _ffn_down_body(te_ref, h_ref, wd_ref, o_ref, wdb_ref):
    @pl.when(_weights_changed(te_ref, pl.program_id(1)))
    def _():
        wdb_ref[...] = wd_ref[...].astype(BF16)

    o_ref[...] = jnp.dot(h_ref[...], wdb_ref[...], preferred_element_type=F32)


def _ffn_down(hmid, wd, tile_expert):
    n_tiles = tile_expert.shape[0]
    n = wd.shape[2]
    grid_spec = pltpu.PrefetchScalarGridSpec(
        num_scalar_prefetch=1,
        grid=(n // TN_DOWN, n_tiles),
        in_specs=[pl.BlockSpec((TM, D_FF), lambda j, t, te: (t, 0)),
                  pl.BlockSpec((None, D_FF, TN_DOWN), lambda j, t, te: (te[t], 0, j))],
        out_specs=pl.BlockSpec((TM, TN_DOWN), lambda j, t, te: (t, j)),
        scratch_shapes=[pltpu.VMEM((D_FF, TN_DOWN), BF16)])
    return pl.pallas_call(
        _ffn_down_body,
        out_shape=jax.ShapeDtypeStruct((n_tiles * TM, n), F32),
        grid_spec=grid_spec,
        compiler_params=_params(2),
        name="ffn_down",
    )(tile_expert, hmid, wd)


def _router_body(x_ref, r_ref, comb_ref):
    logits = jnp.dot(x_ref[...], r_ref[...], preferred_element_type=F32, precision=lax.Precision.HIGHEST)
    lane = lax.broadcasted_iota(jnp.int32, logits.shape, 1).astype(F32)
    n = float(N_EXPERTS)
    m1 = jnp.max(logits, axis=-1, keepdims=True)
    i1 = jnp.min(jnp.where(logits == m1, lane, n), axis=-1, keepdims=True)
    rest = jnp.where(lane == i1, -jnp.inf, logits)
    m2 = jnp.max(rest, axis=-1, keepdims=True)
    i2 = jnp.min(jnp.where(rest == m2, lane, n), axis=-1, keepdims=True)
    e2 = jnp.exp(m2 - m1)
    g1 = 1.0 / (1.0 + e2)
    comb_ref[...] = jnp.where(lane == i1, g1, 0.0) + jnp.where(lane == i2, e2 * g1, 0.0)


def _router(x, r):
    m, k = x.shape
    return pl.pallas_call(
        _router_body,
        out_shape=jax.ShapeDtypeStruct((m, N_EXPERTS), F32),
        grid=(m // TM,),
        in_specs=[pl.BlockSpec((TM, k), lambda i: (i, 0)),
                  pl.BlockSpec((k, N_EXPERTS), lambda i: (0, 0))],
        out_specs=pl.BlockSpec((TM, N_EXPERTS), lambda i: (i, 0)),
        compiler_params=_params(1),
        name="router",
    )(x, r)


def _combine_body(ys_ref, comb_ref, res_ref, g_ref, b_ref, o_ref, ob_ref):
    y = comb_ref[:, 0:1] * ys_ref[0]
    for e in range(1, N_EXPERTS):
        y = y + comb_ref[:, e:e + 1] * ys_ref[e]
    _residual_ln_store(y, res_ref, g_ref, b_ref, o_ref, ob_ref)


def _combine(ys, comb, res, g, b):
    m, n = res.shape
    tm = TM_COMB
    row = lambda i: (i, 0)
    fixed = lambda i: (0, 0)
    return pl.pallas_call(
        _combine_body,
        out_shape=(jax.ShapeDtypeStruct((m, n), F32), jax.ShapeDtypeStruct((m, n), BF16)),
        grid=(m // tm,),
        in_specs=[pl.BlockSpec((N_EXPERTS, tm, n), lambda i: (0, i, 0)),
                  pl.BlockSpec((tm, N_EXPERTS), row),
                  pl.BlockSpec((tm, n), row),
                  pl.BlockSpec((1, n), fixed),
                  pl.BlockSpec((1, n), fixed)],
        out_specs=(pl.BlockSpec((tm, n), row), pl.BlockSpec((tm, n), row)),
        compiler_params=_params(1),
        name="combine",
    )(ys, comb, res, g, b)


def kernel(x_prompt, x_sample, state_a, state_b, state_d, cache_mem_k, cache_mem_v, mem_prompt, w_in, conv_a, pool_w, pool_scale, sg_ln_g, sg_ln_b, sg_w, sg_b, conv_d, cd_ln_g, cd_ln_b, w_out, w_q, w_k, w_v, w_o, ln_g, ln_b, dense_w_gate, dense_w_up, dense_w_down, moe_router, moe_w_gate, moe_w_up, moe_w_down):
    d, g = D_MODEL, D_GRP
    x = jnp.concatenate([x_prompt.reshape(N_PROMPT, d), x_sample.reshape(DEC_BATCH, d)], axis=0)
    xb = x.astype(BF16)
    mem_b = mem_prompt.reshape(BATCH * N_MEM, d).astype(BF16)
    n_tiles = N_TOK // TM
    dense_tiles = (jnp.zeros((n_tiles,), jnp.int32), jnp.arange(n_tiles, dtype=jnp.int32))
    moe_tiles = (jnp.repeat(jnp.arange(N_EXPERTS, dtype=jnp.int32), n_tiles),
                 jnp.tile(jnp.arange(n_tiles, dtype=jnp.int32), N_EXPERTS))
    vec = lambda v: v.reshape(1, -1)

    sa_p, sb_p, sd_p, mk_p, mv_p, sa_s, sb_s, sd_s, sc_s = [], [], [], [], [], [], [], [], []
    for l in range(DEPTH):
        h = _matmul(xb, w_in[l], F32, TM, TN, "w_in")
        y_p, ha, hb, hd = _mix_prompt(h, conv_a[l], pool_w[l], vec(pool_scale[l]), vec(sg_ln_g[l]), vec(sg_ln_b[l]),
                                      sg_w[l], sg_b[l].T, conv_d[l], vec(cd_ln_g[l]), vec(cd_ln_b[l]))
        y_s, ta, tb, td, tc = _mix_sample(
            h, state_a[l].reshape(DEC_BATCH, -1), state_b[l].reshape(DEC_BATCH, -1), state_d[l].reshape(DEC_BATCH, -1),
            conv_a[l], pool_w[l], vec(pool_scale[l]), vec(sg_ln_g[l]), vec(sg_ln_b[l]),
            vec(jnp.repeat(sg_w[l, :, 0, 0], CHUNK)), vec(jnp.repeat(sg_b[l, :, 0], CHUNK)),
            conv_d[l], vec(cd_ln_g[l]), vec(cd_ln_b[l]))
        y_mix = jnp.concatenate([y_p, y_s], axis=0)
        x, xb = _mm_ln_resident(y_mix, w_out[l], x, vec(ln_g[l, 0]), vec(ln_b[l, 0]), "w_out_ln")

        q = _matmul(xb, w_q[l], BF16, TM, TN, "w_q")
        mk = _matmul(mem_b, w_k[l], F32, 512, TN, "w_k")
        mv = _matmul(mem_b, w_v[l], F32, 512, TN, "w_v")
        o_p = _attn_prompt(q, mk, mv)
        o_s = _attn_sample(q[N_PROMPT:].astype(F32).reshape(DEC_BATCH, 1, d),
                           cache_mem_k[l].reshape(DEC_BATCH, N_MEM, d), cache_mem_v[l].reshape(DEC_BATCH, N_MEM, d))
        o = jnp.concatenate([o_p, o_s.reshape(DEC_BATCH, d).astype(BF16)], axis=0)
        x, xb = _mm_ln_resident(o, w_o[l], x, vec(ln_g[l, 1]), vec(ln_b[l, 1]), "w_o_ln")

        j = l // 2
        if l % 2 == 0:
            hmid = _ffn_up(xb, dense_w_gate[j][None], dense_w_up[j][None], *dense_tiles)
            x, xb = _mm_ln_ktiled(hmid, dense_w_down[j], x, vec(ln_g[l, 2]), vec(ln_b[l, 2]), "w_down_ln")
        else:
            comb = _router(x, moe_router[j])
            hmid = _ffn_up(xb, moe_w_gate[j], moe_w_up[j], *moe_tiles)
            ys = _ffn_down(hmid, moe_w_down[j], moe_tiles[0])
            x, xb = _combine(ys.reshape(N_EXPERTS, N_TOK, d), comb, x, vec(ln_g[l, 2]), vec(ln_b[l, 2]))

        sa_p.append(ha); sb_p.append(hb); sd_p.append(hd)
        mk_p.append(mk.reshape(BATCH, N_MEM, N_HEADS_X, D_HEAD_X)); mv_p.append(mv.reshape(BATCH, N_MEM, N_HEADS_X, D_HEAD_X))
        sa_s.append(ta.reshape(DEC_BATCH, CONV_A - 1, g)); sb_s.append(tb.reshape(DEC_BATCH, POOL_HIST, g))
        sd_s.append(td.reshape(DEC_BATCH, CONV_D - 1, g)); sc_s.append(tc.reshape(DEC_BATCH, 1, g))

    return (x[:N_PROMPT].reshape(BATCH, SEQ, d), x[N_PROMPT:].reshape(DEC_BATCH, 1, d),
            jnp.stack(sa_p), jnp.stack(sb_p), jnp.stack(sd_p), jnp.stack(mk_p), jnp.stack(mv_p),
            jnp.stack(sa_s), jnp.stack(sb_s), jnp.stack(sd_s), jnp.stack(sc_s))
```

```python
import functools

import jax
import jax.numpy as jnp
from jax import lax
from jax.experimental import pallas as pl
from jax.experimental.pallas import tpu as pltpu

F32 = jnp.float32
BF16 = jnp.bfloat16

D_MODEL = 2048
BATCH = 4
SEQ = 2048
DEPTH = 4
DEC_BATCH = 128
PAST_LEN = 16384
D_GRP = 512
D_IN = 8 * D_GRP
CONV_A = 3
POOL_WINDOWS = (2, 4, 8, 16)
D_POOL = 128
POOL_HIST = 15
CHUNK = 128
N_HEADS_C = 4
CONV_D = 31
N_MEM = 256
N_HEADS_X = 4
D_HEAD_X = 512
D_FF = 5632
N_EXPERTS = 8
ALPHA = (2 * DEPTH) ** 0.25
LN_EPS = 1e-5
ATTN_SCALE = D_HEAD_X ** -0.5

N_PROMPT = BATCH * SEQ
N_TOK = N_PROMPT + DEC_BATCH

VMEM_LIMIT_BYTES = 52 * 1024 * 1024

TM = 640
TM_LN = 320
TN = 1024
TF = 512
TK_DOWN = 512
TN_DOWN = 512
T_MIX = 256
ROWS_D = 32
HALO_A, HALO_B, HALO_D = 8, 16, 32
S_MIX = 32
TQ = 512
BS_ATT = 4
TM_E = 512
TM_COMB = 320
TOP_K = 2
N_ASSIGN = TOP_K * N_TOK
N_TILES_E = -(-(N_ASSIGN + N_EXPERTS * (TM_E - 1)) // TM_E)
N_ROWS_E = N_TILES_E * TM_E


def _params(n_axes):
    return pltpu.CompilerParams(dimension_semantics=("arbitrary",) * n_axes,
                                vmem_limit_bytes=VMEM_LIMIT_BYTES)


def _layer_norm(x, g, b):
    mu = jnp.mean(x, axis=-1, keepdims=True)
    xc = x - mu
    var = jnp.mean(xc * xc, axis=-1, keepdims=True)
    return xc * lax.rsqrt(var + LN_EPS) * g + b


def _sigmoid(x):
    return 1.0 / (1.0 + jnp.exp(-x))


def _mm_body(x_ref, w_ref, o_ref, wb_ref):
    @pl.when(pl.program_id(1) == 0)
    def _():
        wb_ref[...] = w_ref[...].astype(BF16)

    o_ref[...] = jnp.dot(x_ref[...], wb_ref[...], preferred_element_type=F32).astype(o_ref.dtype)


def _matmul(x, w, layer, out_dtype, tm, tn, name):
    m, k = x.shape
    n = w.shape[2]
    return pl.pallas_call(
        _mm_body,
        out_shape=jax.ShapeDtypeStruct((m, n), out_dtype),
        grid=(n // tn, m // tm),
        in_specs=[pl.BlockSpec((tm, k), lambda j, i: (i, 0)),
                  pl.BlockSpec((None, k, tn), lambda j, i: (layer, 0, j))],
        out_specs=pl.BlockSpec((tm, tn), lambda j, i: (i, j)),
        scratch_shapes=[pltpu.VMEM((k, tn), BF16)],
        compiler_params=_params(2),
        name=name,
    )(x, w)


def _residual_ln_store(acc, res_ref, g_ref, b_ref, o_ref, ob_ref):
    y = _layer_norm(ALPHA * res_ref[...] + acc, g_ref[...], b_ref[...])
    o_ref[...] = y
    ob_ref[...] = y.astype(BF16)


def _mm_ln_resident_body(a_ref, w_ref, res_ref, g_ref, b_ref, o_ref, ob_ref, wb_ref):
    @pl.when(pl.program_id(0) == 0)
    def _():
        wb_ref[...] = w_ref[...].astype(BF16)

    acc = jnp.dot(a_ref[...], wb_ref[...], preferred_element_type=F32)
    _residual_ln_store(acc, res_ref, g_ref, b_ref, o_ref, ob_ref)


def _mm_ln_resident(a, w, layer, res, g, b, name):
    m, k = a.shape
    n = w.shape[2]
    tm = TM_LN
    row = lambda i: (i, 0)
    fixed = lambda i: (0, 0)
    return pl.pallas_call(
        _mm_ln_resident_body,
        out_shape=(jax.ShapeDtypeStruct((m, n), F32), jax.ShapeDtypeStruct((m, n), BF16)),
        grid=(m // tm,),
        in_specs=[pl.BlockSpec((tm, k), row),
                  pl.BlockSpec((None, k, n), lambda i: (layer, 0, 0), pipeline_mode=pl.Buffered(1)),
                  pl.BlockSpec((tm, n), row),
                  pl.BlockSpec((1, n), fixed),
                  pl.BlockSpec((1, n), fixed)],
        out_specs=(pl.BlockSpec((tm, n), row), pl.BlockSpec((tm, n), row)),
        scratch_shapes=[pltpu.VMEM((k, n), BF16)],
        compiler_params=_params(1),
        name=name,
    )(a, w, res, g, b)


def _mm_ln_ktiled_body(a_ref, w_ref, res_ref, g_ref, b_ref, o_ref, ob_ref, acc_ref, *, nk):
    k = pl.program_id(1)
    part = jnp.dot(a_ref[...], w_ref[...].astype(BF16), preferred_element_type=F32)

    @pl.when(k == 0)
    def _():
        acc_ref[...] = part

    @pl.when(k > 0)
    def _():
        acc_ref[...] += part

    @pl.when(k == nk - 1)
    def _():
        _residual_ln_store(acc_ref[...], res_ref, g_ref, b_ref, o_ref, ob_ref)


def _mm_ln_ktiled(a, w, layer, res, g, b, name):
    m, k = a.shape
    n = w.shape[2]
    tm, tk = TM, TK_DOWN
    nk = k // tk
    row = lambda i, kk: (i, 0)
    fixed = lambda i, kk: (0, 0)
    return pl.pallas_call(
        functools.partial(_mm_ln_ktiled_body, nk=nk),
        out_shape=(jax.ShapeDtypeStruct((m, n), F32), jax.ShapeDtypeStruct((m, n), BF16)),
        grid=(m // tm, nk),
        in_specs=[pl.BlockSpec((tm, tk), lambda i, kk: (i, kk)),
                  pl.BlockSpec((None, tk, n), lambda i, kk: (layer, kk, 0)),
                  pl.BlockSpec((tm, n), row),
                  pl.BlockSpec((1, n), fixed),
                  pl.BlockSpec((1, n), fixed)],
        out_specs=(pl.BlockSpec((tm, n), row), pl.BlockSpec((tm, n), row)),
        scratch_shapes=[pltpu.VMEM((tm, n), F32)],
        compiler_params=_params(2),
        name=name,
    )(a, w, res, g, b)


def _mix_prompt_body(h_ref, ca_ref, pw_ref, ps_ref, lg_ref, lb_ref, sw_ref, sbt_ref, cd_ref, dg_ref, db_ref,
                     y_ref, sa_ref, sb_ref, sd_ref, ea_ref, eb_ref, ed_ref, vn_ref, *, n_steps):
    t = T_MIX
    g = D_GRP
    s = pl.program_id(1)

    @pl.when(s == 0)
    def _():
        ea_ref[0:HALO_A, :] = jnp.zeros((HALO_A, g), F32)
        eb_ref[0:HALO_B, :] = jnp.zeros((HALO_B, g), F32)
        ed_ref[0:HALO_D, :] = jnp.zeros((HALO_D, g), F32)

    e = h_ref[:, g:2 * g] * h_ref[:, 2 * g:3 * g]
    ea_ref[HALO_A:HALO_A + t, :] = e
    conv_a = (ca_ref[2:3, :] * e + ca_ref[1:2, :] * ea_ref[HALO_A - 1:HALO_A - 1 + t, :]
              + ca_ref[0:1, :] * ea_ref[HALO_A - 2:HALO_A - 2 + t, :])
    y_ref[:, 0:g] = (h_ref[:, 0:g] * conv_a).astype(BF16)

    eb_ref[HALO_B:HALO_B + t, :] = h_ref[:, 3 * g:4 * g]
    pos = s * t + lax.broadcasted_iota(jnp.int32, (t, 1), 0)
    for gi, w in enumerate(POOL_WINDOWS):
        lo = gi * D_POOL
        cur = h_ref[:, 3 * g + lo:3 * g + lo + D_POOL]
        win = cur
        for k in range(1, w):
            win = win + eb_ref[HALO_B - k:HALO_B - k + t, lo:lo + D_POOL]
        cnt = jnp.minimum(pos + 1, w).astype(F32)
        pooled = win / cnt - cur
        yb = jnp.dot(pooled.astype(BF16), pw_ref[gi].astype(BF16), preferred_element_type=F32)
        y_ref[:, g + lo:g + lo + D_POOL] = (yb * ps_ref[:, lo:lo + D_POOL]).astype(BF16)

    vn_ref[...] = _layer_norm(h_ref[:, 5 * g:6 * g], lg_ref[...], lb_ref[...])
    n_chunks = t // CHUNK
    rows = lax.broadcasted_iota(jnp.int32, (CHUNK, CHUNK), 0)
    cols = lax.broadcasted_iota(jnp.int32, (CHUNK, CHUNK), 1)
    for hh in range(N_HEADS_C):
        lo = hh * CHUNK
        w_tril = jnp.where(rows >= cols, sw_ref[hh], 0.0).astype(BF16)
        rhs = jnp.concatenate([vn_ref[c * CHUNK:(c + 1) * CHUNK, lo:lo + CHUNK] for c in range(n_chunks)], axis=1)
        mixed = jnp.dot(w_tril, rhs.astype(BF16), preferred_element_type=F32)
        bias = sbt_ref[:, hh:hh + 1]
        for c in range(n_chunks):
            u = h_ref[c * CHUNK:(c + 1) * CHUNK, 4 * g + lo:4 * g + lo + CHUNK]
            y_ref[c * CHUNK:(c + 1) * CHUNK, 2 * g + lo:2 * g + lo + CHUNK] = (
                u * (mixed[:, c * CHUNK:(c + 1) * CHUNK] + bias)).astype(BF16)

    ed_ref[HALO_D:HALO_D + t, :] = h_ref[:, 6 * g:7 * g] * _sigmoid(h_ref[:, 7 * g:8 * g])
    for rb in range(t // ROWS_D):
        r0 = HALO_D + rb * ROWS_D
        acc = cd_ref[CONV_D - 1:CONV_D, :] * ed_ref[r0:r0 + ROWS_D, :]
        for k in range(CONV_D - 1):
            off = r0 - (CONV_D - 1) + k
            acc = acc + cd_ref[k:k + 1, :] * ed_ref[off:off + ROWS_D, :]
        z = _layer_norm(acc, dg_ref[...], db_ref[...])
        y_ref[rb * ROWS_D:(rb + 1) * ROWS_D, 3 * g:4 * g] = (z * _sigmoid(z)).astype(BF16)

    @pl.when(s == n_steps - 1)
    def _():
        sa_ref[0] = ea_ref[HALO_A + t - (CONV_A - 1):HALO_A + t, :]
        sb_ref[0] = eb_ref[HALO_B + t - POOL_HIST:HALO_B + t, :]
        sd_ref[0] = ed_ref[HALO_D + t - (CONV_D - 1):HALO_D + t, :]

    ea_ref[0:HALO_A, :] = ea_ref[t:t + HALO_A, :]
    eb_ref[0:HALO_B, :] = eb_ref[t:t + HALO_B, :]
    ed_ref[0:HALO_D, :] = ed_ref[t:t + HALO_D, :]


def _mix_prompt(h, ca, pw, ps, lg, lb, sw, sbt, cd, dg, db):
    n_steps = SEQ // T_MIX
    g = D_GRP
    c2 = lambda b, s: (0, 0)
    c3 = lambda b, s: (0, 0, 0)
    st = lambda b, s: (b, 0, 0)
    return pl.pallas_call(
        functools.partial(_mix_prompt_body, n_steps=n_steps),
        out_shape=(jax.ShapeDtypeStruct((N_PROMPT, D_MODEL), BF16),
                   jax.ShapeDtypeStruct((BATCH, CONV_A - 1, g), F32),
                   jax.ShapeDtypeStruct((BATCH, POOL_HIST, g), F32),
                   jax.ShapeDtypeStruct((BATCH, CONV_D - 1, g), F32)),
        grid=(BATCH, n_steps),
        in_specs=[pl.BlockSpec((T_MIX, D_IN), lambda b, s: (b * n_steps + s, 0)),
                  pl.BlockSpec((CONV_A, g), c2),
                  pl.BlockSpec((len(POOL_WINDOWS), D_POOL, D_POOL), c3),
                  pl.BlockSpec((1, g), c2),
                  pl.BlockSpec((1, g), c2),
                  pl.BlockSpec((1, g), c2),
                  pl.BlockSpec((N_HEADS_C, CHUNK, CHUNK), c3),
                  pl.BlockSpec((CHUNK, N_HEADS_C), c2),
                  pl.BlockSpec((CONV_D, g), c2),
                  pl.BlockSpec((1, g), c2),
                  pl.BlockSpec((1, g), c2)],
        out_specs=(pl.BlockSpec((T_MIX, D_MODEL), lambda b, s: (b * n_steps + s, 0)),
                   pl.BlockSpec((1, CONV_A - 1, g), st),
                   pl.BlockSpec((1, POOL_HIST, g), st),
                   pl.BlockSpec((1, CONV_D - 1, g), st)),
        scratch_shapes=[pltpu.VMEM((HALO_A + T_MIX, g), F32),
                        pltpu.VMEM((HALO_B + T_MIX, g), F32),
                        pltpu.VMEM((HALO_D + T_MIX, g), F32),
                        pltpu.VMEM((T_MIX, g), F32)],
        compiler_params=_params(2),
        name="mix_prompt",
    )(h, ca, pw, ps, lg, lb, sw, sbt, cd, dg, db)


def _mix_sample_body(h_ref, sa_ref, sb_ref, sd_ref, ca_ref, pw_ref, ps_ref, lg_ref, lb_ref, w0_ref, b0_ref,
                     cd_ref, dg_ref, db_ref, y_ref, na_ref, nb_ref, nd_ref, vn_ref):
    g = D_GRP

    e = h_ref[:, g:2 * g] * h_ref[:, 2 * g:3 * g]
    conv_a = ca_ref[2:3, :] * e + ca_ref[1:2, :] * sa_ref[:, g:2 * g] + ca_ref[0:1, :] * sa_ref[:, 0:g]
    y_ref[:, 0:g] = (h_ref[:, 0:g] * conv_a).astype(BF16)
    na_ref[:, 0:g] = sa_ref[:, g:2 * g]
    na_ref[:, g:2 * g] = e

    for gi, w in enumerate(POOL_WINDOWS):
        lo = gi * D_POOL
        cur = h_ref[:, 3 * g + lo:3 * g + lo + D_POOL]
        win = cur
        for k in range(1, w):
            col = (POOL_HIST - k) * g + lo
            win = win + sb_ref[:, col:col + D_POOL]
        cnt = float(min(PAST_LEN + 1, w))
        pooled = win / cnt - cur
        yb = jnp.dot(pooled.astype(BF16), pw_ref[gi].astype(BF16), preferred_element_type=F32)
        y_ref[:, g + lo:g + lo + D_POOL] = (yb * ps_ref[:, lo:lo + D_POOL]).astype(BF16)
    nb_ref[:, 0:(POOL_HIST - 1) * g] = sb_ref[:, g:POOL_HIST * g]
    nb_ref[:, (POOL_HIST - 1) * g:POOL_HIST * g] = h_ref[:, 3 * g:4 * g]

    vn = _layer_norm(h_ref[:, 5 * g:6 * g], lg_ref[...], lb_ref[...])
    vn_ref[...] = vn
    y_ref[:, 2 * g:3 * g] = (h_ref[:, 4 * g:5 * g] * (w0_ref[...] * vn + b0_ref[...])).astype(BF16)

    glu = h_ref[:, 6 * g:7 * g] * _sigmoid(h_ref[:, 7 * g:8 * g])
    acc = cd_ref[CONV_D - 1:CONV_D, :] * glu
    for k in range(CONV_D - 1):
        acc = acc + cd_ref[k:k + 1, :] * sd_ref[:, k * g:(k + 1) * g]
    z = _layer_norm(acc, dg_ref[...], db_ref[...])
    y_ref[:, 3 * g:4 * g] = (z * _sigmoid(z)).astype(BF16)
    nd_ref[:, 0:(CONV_D - 2) * g] = sd_ref[:, g:(CONV_D - 1) * g]
    nd_ref[:, (CONV_D - 2) * g:(CONV_D - 1) * g] = glu


def _mix_sample(h, sa, sb, sd, ca, pw, ps, lg, lb, w0, b0, cd, dg, db):
    g = D_GRP
    n_steps = DEC_BATCH // S_MIX
    first = N_PROMPT // S_MIX
    row = lambda i: (i, 0)
    c2 = lambda i: (0, 0)
    c3 = lambda i: (0, 0, 0)
    wa, wb, wd = (CONV_A - 1) * g, POOL_HIST * g, (CONV_D - 1) * g
    return pl.pallas_call(
        _mix_sample_body,
        out_shape=(jax.ShapeDtypeStruct((DEC_BATCH, D_MODEL), BF16),
                   jax.ShapeDtypeStruct((DEC_BATCH, wa), F32),
                   jax.ShapeDtypeStruct((DEC_BATCH, wb), F32),
                   jax.ShapeDtypeStruct((DEC_BATCH, wd), F32),
                   jax.ShapeDtypeStruct((DEC_BATCH, g), F32)),
        grid=(n_steps,),
        in_specs=[pl.BlockSpec((S_MIX, D_IN), lambda i: (first + i, 0)),
                  pl.BlockSpec((S_MIX, wa), row),
                  pl.BlockSpec((S_MIX, wb), row),
                  pl.BlockSpec((S_MIX, wd), row),
                  pl.BlockSpec((CONV_A, g), c2),
                  pl.BlockSpec((len(POOL_WINDOWS), D_POOL, D_POOL), c3),
                  pl.BlockSpec((1, g), c2),
                  pl.BlockSpec((1, g), c2),
                  pl.BlockSpec((1, g), c2),
                  pl.BlockSpec((1, g), c2),
                  pl.BlockSpec((1, g), c2),
                  pl.BlockSpec((CONV_D, g), c2),
                  pl.BlockSpec((1, g), c2),
                  pl.BlockSpec((1, g), c2)],
        out_specs=(pl.BlockSpec((S_MIX, D_MODEL), row),
                   pl.BlockSpec((S_MIX, wa), row),
                   pl.BlockSpec((S_MIX, wb), row),
                   pl.BlockSpec((S_MIX, wd), row),
                   pl.BlockSpec((S_MIX, g), row)),
        compiler_params=_params(1),
        name="mix_sample",
    )(h, sa, sb, sd, ca, pw, ps, lg, lb, w0, b0, cd, dg, db)


def _attn_prompt_body(q_ref, k_ref, v_ref, o_ref):
    for hh in range(N_HEADS_X):
        sl = slice(hh * D_HEAD_X, (hh + 1) * D_HEAD_X)
        kh = k_ref[:, sl].astype(BF16)
        vh = v_ref[:, sl].astype(BF16)
        s = lax.dot_general(q_ref[:, sl], kh, (((1,), (1,)), ((), ())), preferred_element_type=F32) * ATTN_SCALE
        p = jnp.exp(s - jnp.max(s, axis=-1, keepdims=True))
        p = p / jnp.sum(p, axis=-1, keepdims=True)
        o_ref[:, sl] = jnp.dot(p.astype(BF16), vh, preferred_element_type=F32).astype(BF16)


def _attn_prompt(q, mk, mv):
    n_q = SEQ // TQ
    kv = pl.BlockSpec((N_MEM, D_MODEL), lambda b, i: (b, 0))
    qo = pl.BlockSpec((TQ, D_MODEL), lambda b, i: (b * n_q + i, 0))
    return pl.pallas_call(
        _attn_prompt_body,
        out_shape=jax.ShapeDtypeStruct((N_PROMPT, D_MODEL), BF16),
        grid=(BATCH, n_q),
        in_specs=[qo, kv, kv],
        out_specs=qo,
        compiler_params=_params(2),
        name="attn_prompt",
    )(q, mk, mv)


def _attn_sample_body(q_ref, k_ref, v_ref, o_ref):
    for j in range(BS_ATT):
        s = jnp.sum(k_ref[j] * q_ref[j][None], axis=-1, keepdims=True) * ATTN_SCALE
        p = jnp.exp(s - jnp.max(s, axis=0, keepdims=True))
        p = p / jnp.sum(p, axis=0, keepdims=True)
        o_ref[j] = jnp.sum(p * v_ref[j], axis=0)


def _attn_sample(q, ck, cv, layer):
    qo = pl.BlockSpec((BS_ATT, N_HEADS_X, D_HEAD_X), lambda i: (i, 0, 0))
    kv = pl.BlockSpec((None, BS_ATT, N_MEM, N_HEADS_X, D_HEAD_X), lambda i: (layer, i, 0, 0, 0))
    return pl.pallas_call(
        _attn_sample_body,
        out_shape=jax.ShapeDtypeStruct((DEC_BATCH, N_HEADS_X, D_HEAD_X), F32),
        grid=(DEC_BATCH // BS_ATT,),
        in_specs=[qo, kv, kv],
        out_specs=qo,
        compiler_params=_params(1),
        name="attn_sample",
    )(q, ck, cv)


def _weights_changed(te_ref, t):
    return jnp.logical_or(t == 0, te_ref[t] != te_ref[jnp.maximum(t - 1, 0)])


def _ffn_up_body(te_ref, tv_ref, x_ref, wg_ref, wu_ref, o_ref, wgb_ref, wub_ref):
    t = pl.program_id(1)

    @pl.when(_weights_changed(te_ref, t))
    def _():
        wgb_ref[...] = wg_ref[...].astype(BF16)
        wub_ref[...] = wu_ref[...].astype(BF16)

    @pl.when(tv_ref[t] != 0)
    def _():
        x = x_ref[...]
        gate = jnp.dot(x, wgb_ref[...], preferred_element_type=F32)
        up = jnp.dot(x, wub_ref[...], preferred_element_type=F32)
        o_ref[...] = (gate * _sigmoid(gate) * up).astype(BF16)

    @pl.when(tv_ref[t] == 0)
    def _():
        o_ref[...] = jnp.zeros(o_ref.shape, BF16)


def _ffn_up(x, wg, wu, tile_expert, tile_valid, tm):
    n_tiles = tile_expert.shape[0]
    k = x.shape[1]
    grid_spec = pltpu.PrefetchScalarGridSpec(
        num_scalar_prefetch=2,
        grid=(D_FF // TF, n_tiles),
        in_specs=[pl.BlockSpec((tm, k), lambda f, t, te, tv: (t, 0)),
                  pl.BlockSpec((None, k, TF), lambda f, t, te, tv: (te[t], 0, f)),
                  pl.BlockSpec((None, k, TF), lambda f, t, te, tv: (te[t], 0, f))],
        out_specs=pl.BlockSpec((tm, TF), lambda f, t, te, tv: (t, f)),
        scratch_shapes=[pltpu.VMEM((k, TF), BF16), pltpu.VMEM((k, TF), BF16)])
    return pl.pallas_call(
        _ffn_up_body,
        out_shape=jax.ShapeDtypeStruct((n_tiles * tm, D_FF), BF16),
        grid_spec=grid_spec,
        compiler_params=_params(2),
        name="ffn_up",
    )(tile_expert, tile_valid, x, wg, wu)


def _ffn_down_body(te_ref, tv_ref, h_ref, wd_ref, o_ref, wdb_ref):
    t = pl.program_id(1)

    @pl.when(_weights_changed(te_ref, t))
    def _():
        wdb_ref[...] = wd_ref[...].astype(BF16)

    @pl.when(tv_ref[t] != 0)
    def _():
        o_ref[...] = jnp.dot(h_ref[...], wdb_ref[...], preferred_element_type=F32)

    @pl.when(tv_ref[t] == 0)
    def _():
        o_ref[...] = jnp.zeros(o_ref.shape, F32)


def _ffn_down(hmid, wd, tile_expert, tile_valid, tm):
    n_tiles = tile_expert.shape[0]
    n = wd.shape[2]
    grid_spec = pltpu.PrefetchScalarGridSpec(
        num_scalar_prefetch=2,
        grid=(n // TN_DOWN, n_tiles),
        in_specs=[pl.BlockSpec((tm, D_FF), lambda j, t, te, tv: (t, 0)),
                  pl.BlockSpec((None, D_FF, TN_DOWN), lambda j, t, te, tv: (te[t], 0, j))],
        out_specs=pl.BlockSpec((tm, TN_DOWN), lambda j, t, te, tv: (t, j)),
        scratch_shapes=[pltpu.VMEM((D_FF, TN_DOWN), BF16)])
    return pl.pallas_call(
        _ffn_down_body,
        out_shape=jax.ShapeDtypeStruct((n_tiles * tm, n), F32),
        grid_spec=grid_spec,
        compiler_params=_params(2),
        name="ffn_down",
    )(tile_expert, tile_valid, hmid, wd)


def _router_body(x_ref, r_ref, idx_ref, gate_ref):
    logits = jnp.dot(x_ref[...], r_ref[...], preferred_element_type=F32, precision=lax.Precision.HIGHEST)
    lane = lax.broadcasted_iota(jnp.int32, logits.shape, 1).astype(F32)
    n = float(N_EXPERTS)
    m1 = jnp.max(logits, axis=-1, keepdims=True)
    i1 = jnp.min(jnp.where(logits == m1, lane, n), axis=-1, keepdims=True)
    rest = jnp.where(lane == i1, -jnp.inf, logits)
    m2 = jnp.max(rest, axis=-1, keepdims=True)
    i2 = jnp.min(jnp.where(rest == m2, lane, n), axis=-1, keepdims=True)
    e2 = jnp.exp(m2 - m1)
    g1 = 1.0 / (1.0 + e2)
    first = lax.broadcasted_iota(jnp.int32, idx_ref.shape, 1) == 0
    idx_ref[...] = jnp.where(first, i1, i2).astype(jnp.int32)
    gate_ref[...] = jnp.where(first, g1, e2 * g1)


def _router(x, r):
    m, k = x.shape
    out = pl.BlockSpec((TM, TOP_K), lambda i: (i, 0))
    return pl.pallas_call(
        _router_body,
        out_shape=(jax.ShapeDtypeStruct((m, TOP_K), jnp.int32), jax.ShapeDtypeStruct((m, TOP_K), F32)),
        grid=(m // TM,),
        in_specs=[pl.BlockSpec((TM, k), lambda i: (i, 0)),
                  pl.BlockSpec((k, N_EXPERTS), lambda i: (0, 0))],
        out_specs=(out, out),
        compiler_params=_params(1),
        name="router",
    )(x, r)


def _route_tables(idx, expert_base):
    a = idx.reshape(-1)
    onehot = (a[:, None] == jnp.arange(N_EXPERTS, dtype=jnp.int32)[None, :]).astype(jnp.int32)
    csum = jnp.cumsum(onehot, axis=0)
    rank = jnp.sum(onehot * csum, axis=1) - 1
    counts = csum[-1]
    padded = (counts + (TM_E - 1)) // TM_E * TM_E
    ends = jnp.cumsum(padded)
    row = jnp.sum(onehot * (ends - padded)[None, :], axis=1) + rank
    row_token = jnp.zeros((N_ROWS_E,), jnp.int32).at[row].set(jnp.arange(N_ASSIGN, dtype=jnp.int32) // TOP_K)
    tile_start = jnp.arange(N_TILES_E, dtype=jnp.int32) * TM_E
    tile_expert = jnp.sum((tile_start[:, None] >= ends[None, :]).astype(jnp.int32), axis=1)
    tile_expert = jnp.minimum(tile_expert, N_EXPERTS - 1) + expert_base
    tile_valid = (tile_start < ends[-1]).astype(jnp.int32)
    return row, row_token, tile_expert, tile_valid


def _start_row_copies(src_hbm, rows_ref, first, stride, dst, sem, n):
    def body(r, carry):
        src_row = rows_ref[first + stride * r]
        pltpu.make_async_copy(src_hbm.at[pl.ds(src_row, 1)], dst.at[pl.ds(r, 1)], sem).start()
        return carry

    lax.fori_loop(0, n, body, 0)


def _wait_row_copies(src_hbm, dst, sem, n):
    pltpu.make_async_copy(src_hbm.at[pl.ds(0, n)], dst, sem).wait()


def _gather_rows_body(tok_ref, x_hbm, o_ref, buf, sem):
    t = pl.program_id(0)
    n_tiles = pl.num_programs(0)

    @pl.when(t == 0)
    def _():
        _start_row_copies(x_hbm, tok_ref, 0, 1, buf.at[0], sem.at[0], TM_E)

    @pl.when(t + 1 < n_tiles)
    def _():
        nxt = (t + 1) % 2
        _start_row_copies(x_hbm, tok_ref, (t + 1) * TM_E, 1, buf.at[nxt], sem.at[nxt], TM_E)

    slot = t % 2
    _wait_row_copies(x_hbm, buf.at[slot], sem.at[slot], TM_E)
    o_ref[...] = buf[slot].astype(BF16)


def _gather_rows(x, row_token):
    d = x.shape[1]
    grid_spec = pltpu.PrefetchScalarGridSpec(
        num_scalar_prefetch=1,
        grid=(N_TILES_E,),
        in_specs=[pl.BlockSpec(memory_space=pl.ANY)],
        out_specs=pl.BlockSpec((TM_E, d), lambda t, tok: (t, 0)),
        scratch_shapes=[pltpu.VMEM((2, TM_E, d), F32), pltpu.SemaphoreType.DMA((2,))])
    return pl.pallas_call(
        _gather_rows_body,
        out_shape=jax.ShapeDtypeStruct((N_ROWS_E, d), BF16),
        grid_spec=grid_spec,
        compiler_params=_params(1),
        name="gather_rows",
    )(row_token, x)


def _combine_body(row_ref, ys_hbm, gate_ref, res_ref, g_ref, b_ref, o_ref, ob_ref, buf, sem):
    t = pl.program_id(0)
    n_tiles = pl.num_programs(0)
    tm = TM_COMB

    def start(tile, slot):
        for k in range(TOP_K):
            _start_row_copies(ys_hbm, row_ref, TOP_K * tile * tm + k, TOP_K, buf.at[slot, k], sem.at[slot], tm)

    @pl.when(t == 0)
    def _():
        start(0, 0)

    @pl.when(t + 1 < n_tiles)
    def _():
        start(t + 1, (t + 1) % 2)

    slot = t % 2
    for k in range(TOP_K):
        _wait_row_copies(ys_hbm, buf.at[slot, k], sem.at[slot], tm)
    y = gate_ref[:, 0:1] * buf[slot, 0] + gate_ref[:, 1:2] * buf[slot, 1]
    _residual_ln_store(y, res_ref, g_ref, b_ref, o_ref, ob_ref)


def _combine(ys, row, gate, res, g, b):
    m, n = res.shape
    tm = TM_COMB
    rows = lambda i, r: (i, 0)
    fixed = lambda i, r: (0, 0)
    grid_spec = pltpu.PrefetchScalarGridSpec(
        num_scalar_prefetch=1,
        grid=(m // tm,),
        in_specs=[pl.BlockSpec(memory_space=pl.ANY),
                  pl.BlockSpec((tm, TOP_K), rows),
                  pl.BlockSpec((tm, n), rows),
                  pl.BlockSpec((1, n), fixed),
                  pl.BlockSpec((1, n), fixed)],
        out_specs=(pl.BlockSpec((tm, n), rows), pl.BlockSpec((tm, n), rows)),
        scratch_shapes=[pltpu.VMEM((2, TOP_K, tm, n), F32), pltpu.SemaphoreType.DMA((2,))])
    return pl.pallas_call(
        _combine_body,
        out_shape=(jax.ShapeDtypeStruct((m, n), F32), jax.ShapeDtypeStruct((m, n), BF16)),
        grid_spec=grid_spec,
        compiler_params=_params(1),
        name="combine",
    )(row, ys, gate, res, g, b)


def kernel(x_prompt, x_sample, state_a, state_b, state_d, cache_mem_k, cache_mem_v, mem_prompt, w_in, conv_a, pool_w, pool_scale, sg_ln_g, sg_ln_b, sg_w, sg_b, conv_d, cd_ln_g, cd_ln_b, w_out, w_q, w_k, w_v, w_o, ln_g, ln_b, dense_w_gate, dense_w_up, dense_w_down, moe_router, moe_w_gate, moe_w_up, moe_w_down):
    d, g = D_MODEL, D_GRP
    x = jnp.concatenate([x_prompt.reshape(N_PROMPT, d), x_sample.reshape(DEC_BATCH, d)], axis=0)
    xb = x.astype(BF16)
    mem_b = mem_prompt.reshape(BATCH * N_MEM, d).astype(BF16)
    n_tiles = N_TOK // TM
    all_valid = jnp.ones((n_tiles,), jnp.int32)
    moe_wg = moe_w_gate.reshape(-1, d, D_FF)
    moe_wu = moe_w_up.reshape(-1, d, D_FF)
    moe_wd = moe_w_down.reshape(-1, D_FF, d)
    vec = lambda v: v.reshape(1, -1)

    sa_p, sb_p, sd_p, mk_p, mv_p, sa_s, sb_s, sd_s, sc_s = [], [], [], [], [], [], [], [], []
    for l in range(DEPTH):
        h = _matmul(xb, w_in, l, F32, TM, TN, "w_in")
        y_p, ha, hb, hd = _mix_prompt(h, conv_a[l], pool_w[l], vec(pool_scale[l]), vec(sg_ln_g[l]), vec(sg_ln_b[l]),
                                      sg_w[l], sg_b[l].T, conv_d[l], vec(cd_ln_g[l]), vec(cd_ln_b[l]))
        y_s, ta, tb, td, tc = _mix_sample(
            h, state_a[l].reshape(DEC_BATCH, -1), state_b[l].reshape(DEC_BATCH, -1), state_d[l].reshape(DEC_BATCH, -1),
            conv_a[l], pool_w[l], vec(pool_scale[l]), vec(sg_ln_g[l]), vec(sg_ln_b[l]),
            vec(jnp.repeat(sg_w[l, :, 0, 0], CHUNK)), vec(jnp.repeat(sg_b[l, :, 0], CHUNK)),
            conv_d[l], vec(cd_ln_g[l]), vec(cd_ln_b[l]))
        y_mix = jnp.concatenate([y_p, y_s], axis=0)
        x, xb = _mm_ln_resident(y_mix, w_out, l, x, vec(ln_g[l, 0]), vec(ln_b[l, 0]), "w_out_ln")

        q = _matmul(xb, w_q, l, BF16, TM, TN, "w_q")
        mk = _matmul(mem_b, w_k, l, F32, 512, TN, "w_k")
        mv = _matmul(mem_b, w_v, l, F32, 512, TN, "w_v")
        o_p = _attn_prompt(q, mk, mv)
        o_s = _attn_sample(q[N_PROMPT:].astype(F32).reshape(DEC_BATCH, N_HEADS_X, D_HEAD_X), cache_mem_k, cache_mem_v, l)
        o = jnp.concatenate([o_p, o_s.reshape(DEC_BATCH, d).astype(BF16)], axis=0)
        x, xb = _mm_ln_resident(o, w_o, l, x, vec(ln_g[l, 1]), vec(ln_b[l, 1]), "w_o_ln")

        j = l // 2
        if l % 2 == 0:
            hmid = _ffn_up(xb, dense_w_gate, dense_w_up, jnp.full((n_tiles,), j, jnp.int32), all_valid, TM)
            x, xb = _mm_ln_ktiled(hmid, dense_w_down, j, x, vec(ln_g[l, 2]), vec(ln_b[l, 2]), "w_down_ln")
        else:
            idx, gate = _router(x, moe_router[j])
            row, row_token, tile_expert, tile_valid = _route_tables(idx, j * N_EXPERTS)
            xs = _gather_rows(x, row_token)
            hmid = _ffn_up(xs, moe_wg, moe_wu, tile_expert, tile_valid, TM_E)
            ys = _ffn_down(hmid, moe_wd, tile_expert, tile_valid, TM_E)
            x, xb = _combine(ys, row, gate, x, vec(ln_g[l, 2]), vec(ln_b[l, 2]))

        sa_p.append(ha); sb_p.append(hb); sd_p.append(hd)
        mk_p.append(mk.reshape(BATCH, N_MEM, N_HEADS_X, D_HEAD_X)); mv_p.append(mv.reshape(BATCH, N_MEM, N_HEADS_X, D_HEAD_X))
        sa_s.append(ta.reshape(DEC_BATCH, CONV_A - 1, g)); sb_s.append(tb.reshape(DEC_BATCH, POOL_HIST, g))
        sd_s.append(td.reshape(DEC_BATCH, CONV_D - 1, g)); sc_s.append(tc.reshape(DEC_BATCH, 1, g))

    return (x[:N_PROMPT].reshape(BATCH, SEQ, d), x[N_PROMPT:].reshape(DEC_BATCH, 1, d),
            jnp.stack(sa_p), jnp.stack(sb_p), jnp.stack(sd_p), jnp.stack(mk_p), jnp.stack(mv_p),
            jnp.stack(sa_s), jnp.stack(sb_s), jnp.stack(sd_s), jnp.stack(sc_s))
```

```python
import functools

import jax
import jax.numpy as jnp
from jax import lax
from jax.experimental import pallas as pl
from jax.experimental.pallas import tpu as pltpu

F32 = jnp.float32
BF16 = jnp.bfloat16

D_MODEL = 2048
BATCH = 4
SEQ = 2048
DEPTH = 4
DEC_BATCH = 128
PAST_LEN = 16384
D_GRP = 512
D_IN = 8 * D_GRP
CONV_A = 3
POOL_WINDOWS = (2, 4, 8, 16)
D_POOL = 128
POOL_HIST = 15
CHUNK = 128
N_HEADS_C = 4
CONV_D = 31
N_MEM = 256
N_HEADS_X = 4
D_HEAD_X = 512
D_FF = 5632
N_EXPERTS = 8
ALPHA = (2 * DEPTH) ** 0.25
LN_EPS = 1e-5
ATTN_SCALE = D_HEAD_X ** -0.5

N_PROMPT = BATCH * SEQ
N_TOK = N_PROMPT + DEC_BATCH

VMEM_LIMIT_BYTES = 52 * 1024 * 1024

TM = 640
TM_LN = 320
TN = 1024
TF = 512
LN_ROW_GROUPS = 2
LANES = 128
TN_DOWN = 512
T_MIX = 256
ROWS_D = 32
HALO_A, HALO_B, HALO_D = 8, 16, 32
S_MIX = 32
TQ = 512
BS_ATT = 4
TM_E = 512
TM_COMB = 320
TOP_K = 2
N_ASSIGN = TOP_K * N_TOK
N_TILES_E = -(-(N_ASSIGN + N_EXPERTS * (TM_E - 1)) // TM_E)
N_ROWS_E = N_TILES_E * TM_E


def _params(n_axes):
    return pltpu.CompilerParams(dimension_semantics=("arbitrary",) * n_axes,
                                vmem_limit_bytes=VMEM_LIMIT_BYTES)


def _layer_norm(x, g, b):
    mu = jnp.mean(x, axis=-1, keepdims=True)
    xc = x - mu
    var = jnp.mean(xc * xc, axis=-1, keepdims=True)
    return xc * lax.rsqrt(var + LN_EPS) * g + b


def _sigmoid(x):
    return 1.0 / (1.0 + jnp.exp(-x))


def _mm_body(x_ref, w_ref, o_ref, wb_ref):
    @pl.when(pl.program_id(1) == 0)
    def _():
        wb_ref[...] = w_ref[...].astype(BF16)

    o_ref[...] = jnp.dot(x_ref[...], wb_ref[...], preferred_element_type=F32).astype(o_ref.dtype)


def _matmul(x, w, layer, out_dtype, tm, tn, name):
    m, k = x.shape
    n = w.shape[2]
    return pl.pallas_call(
        _mm_body,
        out_shape=jax.ShapeDtypeStruct((m, n), out_dtype),
        grid=(n // tn, m // tm),
        in_specs=[pl.BlockSpec((tm, k), lambda j, i: (i, 0)),
                  pl.BlockSpec((None, k, tn), lambda j, i: (layer, 0, j))],
        out_specs=pl.BlockSpec((tm, tn), lambda j, i: (i, j)),
        scratch_shapes=[pltpu.VMEM((k, tn), BF16)],
        compiler_params=_params(2),
        name=name,
    )(x, w)


def _residual_ln_store(acc, res_ref, g_ref, b_ref, o_ref, ob_ref):
    y = _layer_norm(ALPHA * res_ref[...] + acc, g_ref[...], b_ref[...])
    o_ref[...] = y
    ob_ref[...] = y.astype(BF16)


def _mm_ln_resident_body(a_ref, w_ref, res_ref, g_ref, b_ref, o_ref, ob_ref, *rest):
    wb_ref = rest[-1]
    rows3d_ref = rest[0] if len(rest) == 2 else None

    @pl.when(pl.program_id(0) == 0)
    def _():
        wb_ref[...] = w_ref[...].astype(BF16)

    rows_per_group = a_ref.shape[0] // LN_ROW_GROUPS
    for s in range(LN_ROW_GROUPS):
        rows = slice(s * rows_per_group, (s + 1) * rows_per_group)
        acc = jnp.dot(a_ref[rows, :], wb_ref[...], preferred_element_type=F32)
        y = _layer_norm(ALPHA * res_ref[rows, :] + acc, g_ref[...], b_ref[...])
        o_ref[rows, :] = y
        ob_ref[rows, :] = y.astype(BF16)
        if rows3d_ref is not None:
            for c in range(y.shape[1] // LANES):
                rows3d_ref[rows, c, :] = y[:, c * LANES:(c + 1) * LANES]


def _mm_ln_resident(a, w, layer, res, g, b, name, with_rows3d=False):
    m, k = a.shape
    n = w.shape[2]
    tm = TM_LN
    row = lambda i: (i, 0)
    fixed = lambda i: (0, 0)
    out_shape = [jax.ShapeDtypeStruct((m, n), F32), jax.ShapeDtypeStruct((m, n), BF16)]
    out_specs = [pl.BlockSpec((tm, n), row), pl.BlockSpec((tm, n), row)]
    if with_rows3d:
        out_shape.append(jax.ShapeDtypeStruct((m, n // LANES, LANES), F32))
        out_specs.append(pl.BlockSpec((tm, n // LANES, LANES), lambda i: (i, 0, 0)))
    return pl.pallas_call(
        _mm_ln_resident_body,
        out_shape=tuple(out_shape),
        grid=(m // tm,),
        in_specs=[pl.BlockSpec((tm, k), row),
                  pl.BlockSpec((None, k, n), lambda i: (layer, 0, 0), pipeline_mode=pl.Buffered(1)),
                  pl.BlockSpec((tm, n), row),
                  pl.BlockSpec((1, n), fixed),
                  pl.BlockSpec((1, n), fixed)],
        out_specs=tuple(out_specs),
        scratch_shapes=[pltpu.VMEM((k, n), BF16)],
        compiler_params=_params(1),
        name=name,
    )(a, w, res, g, b)


def _add_ln_body(y_ref, res_ref, g_ref, b_ref, o_ref, ob_ref):
    _residual_ln_store(y_ref[...], res_ref, g_ref, b_ref, o_ref, ob_ref)


def _add_ln(y, res, g, b):
    m, n = res.shape
    tm = TM_LN
    row = lambda i: (i, 0)
    fixed = lambda i: (0, 0)
    return pl.pallas_call(
        _add_ln_body,
        out_shape=(jax.ShapeDtypeStruct((m, n), F32), jax.ShapeDtypeStruct((m, n), BF16)),
        grid=(m // tm,),
        in_specs=[pl.BlockSpec((tm, n), row), pl.BlockSpec((tm, n), row),
                  pl.BlockSpec((1, n), fixed), pl.BlockSpec((1, n), fixed)],
        out_specs=(pl.BlockSpec((tm, n), row), pl.BlockSpec((tm, n), row)),
        compiler_params=_params(1),
        name="add_ln",
    )(y, res, g, b)


def _mix_prompt_body(y_all_ref, h_ref, ca_ref, pw_ref, ps_ref, lg_ref, lb_ref, sw_ref, sbt_ref, cd_ref, dg_ref, db_ref,
                     y_ref, sa_ref, sb_ref, sd_ref, ea_ref, eb_ref, ed_ref, vn_ref, *, n_steps):
    del y_all_ref
    t = T_MIX
    g = D_GRP
    s = pl.program_id(1)

    @pl.when(s == 0)
    def _():
        ea_ref[0:HALO_A, :] = jnp.zeros((HALO_A, g), F32)
        eb_ref[0:HALO_B, :] = jnp.zeros((HALO_B, g), F32)
        ed_ref[0:HALO_D, :] = jnp.zeros((HALO_D, g), F32)

    e = h_ref[:, g:2 * g] * h_ref[:, 2 * g:3 * g]
    ea_ref[HALO_A:HALO_A + t, :] = e
    conv_a = (ca_ref[2:3, :] * e + ca_ref[1:2, :] * ea_ref[HALO_A - 1:HALO_A - 1 + t, :]
              + ca_ref[0:1, :] * ea_ref[HALO_A - 2:HALO_A - 2 + t, :])
    y_ref[:, 0:g] = (h_ref[:, 0:g] * conv_a).astype(BF16)

    eb_ref[HALO_B:HALO_B + t, :] = h_ref[:, 3 * g:4 * g]
    pos = s * t + lax.broadcasted_iota(jnp.int32, (t, 1), 0)
    for gi, w in enumerate(POOL_WINDOWS):
        lo = gi * D_POOL
        cur = h_ref[:, 3 * g + lo:3 * g + lo + D_POOL]
        win = cur
        for k in range(1, w):
            win = win + eb_ref[HALO_B - k:HALO_B - k + t, lo:lo + D_POOL]
        cnt = jnp.minimum(pos + 1, w).astype(F32)
        pooled = win / cnt - cur
        yb = jnp.dot(pooled.astype(BF16), pw_ref[gi].astype(BF16), preferred_element_type=F32)
        y_ref[:, g + lo:g + lo + D_POOL] = (yb * ps_ref[:, lo:lo + D_POOL]).astype(BF16)

    vn_ref[...] = _layer_norm(h_ref[:, 5 * g:6 * g], lg_ref[...], lb_ref[...])
    n_chunks = t // CHUNK
    rows = lax.broadcasted_iota(jnp.int32, (CHUNK, CHUNK), 0)
    cols = lax.broadcasted_iota(jnp.int32, (CHUNK, CHUNK), 1)
    for hh in range(N_HEADS_C):
        lo = hh * CHUNK
        w_tril = jnp.where(rows >= cols, sw_ref[hh], 0.0).astype(BF16)
        rhs = jnp.concatenate([vn_ref[c * CHUNK:(c + 1) * CHUNK, lo:lo + CHUNK] for c in range(n_chunks)], axis=1)
        mixed = jnp.dot(w_tril, rhs.astype(BF16), preferred_element_type=F32)
        bias = sbt_ref[:, hh:hh + 1]
        for c in range(n_chunks):
            u = h_ref[c * CHUNK:(c + 1) * CHUNK, 4 * g + lo:4 * g + lo + CHUNK]
            y_ref[c * CHUNK:(c + 1) * CHUNK, 2 * g + lo:2 * g + lo + CHUNK] = (
                u * (mixed[:, c * CHUNK:(c + 1) * CHUNK] + bias)).astype(BF16)

    ed_ref[HALO_D:HALO_D + t, :] = h_ref[:, 6 * g:7 * g] * _sigmoid(h_ref[:, 7 * g:8 * g])
    for rb in range(t // ROWS_D):
        r0 = HALO_D + rb * ROWS_D
        acc = cd_ref[CONV_D - 1:CONV_D, :] * ed_ref[r0:r0 + ROWS_D, :]
        for k in range(CONV_D - 1):
            off = r0 - (CONV_D - 1) + k
            acc = acc + cd_ref[k:k + 1, :] * ed_ref[off:off + ROWS_D, :]
        z = _layer_norm(acc, dg_ref[...], db_ref[...])
        y_ref[rb * ROWS_D:(rb + 1) * ROWS_D, 3 * g:4 * g] = (z * _sigmoid(z)).astype(BF16)

    @pl.when(s == n_steps - 1)
    def _():
        sa_ref[0] = ea_ref[HALO_A + t - (CONV_A - 1):HALO_A + t, :]
        sb_ref[0] = eb_ref[HALO_B + t - POOL_HIST:HALO_B + t, :]
        sd_ref[0] = ed_ref[HALO_D + t - (CONV_D - 1):HALO_D + t, :]

    ea_ref[0:HALO_A, :] = ea_ref[t:t + HALO_A, :]
    eb_ref[0:HALO_B, :] = eb_ref[t:t + HALO_B, :]
    ed_ref[0:HALO_D, :] = ed_ref[t:t + HALO_D, :]


def _mix_prompt(y_all, h, ca, pw, ps, lg, lb, sw, sbt, cd, dg, db):
    n_steps = SEQ // T_MIX
    g = D_GRP
    c2 = lambda b, s: (0, 0)
    c3 = lambda b, s: (0, 0, 0)
    st = lambda b, s: (b, 0, 0)
    return pl.pallas_call(
        functools.partial(_mix_prompt_body, n_steps=n_steps),
        out_shape=(jax.ShapeDtypeStruct((N_TOK, D_MODEL), BF16),
                   jax.ShapeDtypeStruct((BATCH, CONV_A - 1, g), F32),
                   jax.ShapeDtypeStruct((BATCH, POOL_HIST, g), F32),
                   jax.ShapeDtypeStruct((BATCH, CONV_D - 1, g), F32)),
        grid=(BATCH, n_steps),
        input_output_aliases={0: 0},
        in_specs=[pl.BlockSpec(memory_space=pl.ANY),
                  pl.BlockSpec((T_MIX, D_IN), lambda b, s: (b * n_steps + s, 0)),
                  pl.BlockSpec((CONV_A, g), c2),
                  pl.BlockSpec((len(POOL_WINDOWS), D_POOL, D_POOL), c3),
                  pl.BlockSpec((1, g), c2),
                  pl.BlockSpec((1, g), c2),
                  pl.BlockSpec((1, g), c2),
                  pl.BlockSpec((N_HEADS_C, CHUNK, CHUNK), c3),
                  pl.BlockSpec((CHUNK, N_HEADS_C), c2),
                  pl.BlockSpec((CONV_D, g), c2),
                  pl.BlockSpec((1, g), c2),
                  pl.BlockSpec((1, g), c2)],
        out_specs=(pl.BlockSpec((T_MIX, D_MODEL), lambda b, s: (b * n_steps + s, 0)),
                   pl.BlockSpec((1, CONV_A - 1, g), st),
                   pl.BlockSpec((1, POOL_HIST, g), st),
                   pl.BlockSpec((1, CONV_D - 1, g), st)),
        scratch_shapes=[pltpu.VMEM((HALO_A + T_MIX, g), F32),
                        pltpu.VMEM((HALO_B + T_MIX, g), F32),
                        pltpu.VMEM((HALO_D + T_MIX, g), F32),
                        pltpu.VMEM((T_MIX, g), F32)],
        compiler_params=_params(2),
        name="mix_prompt",
    )(y_all, h, ca, pw, ps, lg, lb, sw, sbt, cd, dg, db)


def _mix_sample_body(y_all_ref, h_ref, sa_ref, sb_ref, sd_ref, ca_ref, pw_ref, ps_ref, lg_ref, lb_ref, w0_ref, b0_ref,
                     cd_ref, dg_ref, db_ref, y_ref, na_ref, nb_ref, nd_ref, vn_ref):
    del y_all_ref
    g = D_GRP

    e = h_ref[:, g:2 * g] * h_ref[:, 2 * g:3 * g]
    conv_a = ca_ref[2:3, :] * e + ca_ref[1:2, :] * sa_ref[:, g:2 * g] + ca_ref[0:1, :] * sa_ref[:, 0:g]
    y_ref[:, 0:g] = (h_ref[:, 0:g] * conv_a).astype(BF16)
    na_ref[:, 0:g] = sa_ref[:, g:2 * g]
    na_ref[:, g:2 * g] = e

    for gi, w in enumerate(POOL_WINDOWS):
        lo = gi * D_POOL
        cur = h_ref[:, 3 * g + lo:3 * g + lo + D_POOL]
        win = cur
        for k in range(1, w):
            col = (POOL_HIST - k) * g + lo
            win = win + sb_ref[:, col:col + D_POOL]
        cnt = float(min(PAST_LEN + 1, w))
        pooled = win / cnt - cur
        yb = jnp.dot(pooled.astype(BF16), pw_ref[gi].astype(BF16), preferred_element_type=F32)
        y_ref[:, g + lo:g + lo + D_POOL] = (yb * ps_ref[:, lo:lo + D_POOL]).astype(BF16)
    nb_ref[:, 0:(POOL_HIST - 1) * g] = sb_ref[:, g:POOL_HIST * g]
    nb_ref[:, (POOL_HIST - 1) * g:POOL_HIST * g] = h_ref[:, 3 * g:4 * g]

    vn = _layer_norm(h_ref[:, 5 * g:6 * g], lg_ref[...], lb_ref[...])
    vn_ref[...] = vn
    y_ref[:, 2 * g:3 * g] = (h_ref[:, 4 * g:5 * g] * (w0_ref[...] * vn + b0_ref[...])).astype(BF16)

    glu = h_ref[:, 6 * g:7 * g] * _sigmoid(h_ref[:, 7 * g:8 * g])
    acc = cd_ref[CONV_D - 1:CONV_D, :] * glu
    for k in range(CONV_D - 1):
        acc = acc + cd_ref[k:k + 1, :] * sd_ref[:, k * g:(k + 1) * g]
    z = _layer_norm(acc, dg_ref[...], db_ref[...])
    y_ref[:, 3 * g:4 * g] = (z * _sigmoid(z)).astype(BF16)
    nd_ref[:, 0:(CONV_D - 2) * g] = sd_ref[:, g:(CONV_D - 1) * g]
    nd_ref[:, (CONV_D - 2) * g:(CONV_D - 1) * g] = glu


def _mix_sample(y_all, h, sa, sb, sd, ca, pw, ps, lg, lb, w0, b0, cd, dg, db):
    g = D_GRP
    n_steps = DEC_BATCH // S_MIX
    first = N_PROMPT // S_MIX
    row = lambda i: (i, 0)
    c2 = lambda i: (0, 0)
    c3 = lambda i: (0, 0, 0)
    wa, wb, wd = (CONV_A - 1) * g, POOL_HIST * g, (CONV_D - 1) * g
    return pl.pallas_call(
        _mix_sample_body,
        out_shape=(jax.ShapeDtypeStruct((N_TOK, D_MODEL), BF16),
                   jax.ShapeDtypeStruct((DEC_BATCH, wa), F32),
                   jax.ShapeDtypeStruct((DEC_BATCH, wb), F32),
                   jax.ShapeDtypeStruct((DEC_BATCH, wd), F32),
                   jax.ShapeDtypeStruct((DEC_BATCH, g), F32)),
        grid=(n_steps,),
        input_output_aliases={0: 0},
        in_specs=[pl.BlockSpec(memory_space=pl.ANY),
                  pl.BlockSpec((S_MIX, D_IN), lambda i: (first + i, 0)),
                  pl.BlockSpec((S_MIX, wa), row),
                  pl.BlockSpec((S_MIX, wb), row),
                  pl.BlockSpec((S_MIX, wd), row),
                  pl.BlockSpec((CONV_A, g), c2),
                  pl.BlockSpec((len(POOL_WINDOWS), D_POOL, D_POOL), c3),
                  pl.BlockSpec((1, g), c2),
                  pl.BlockSpec((1, g), c2),
                  pl.BlockSpec((1, g), c2),
                  pl.BlockSpec((1, g), c2),
                  pl.BlockSpec((1, g), c2),
                  pl.BlockSpec((CONV_D, g), c2),
                  pl.BlockSpec((1, g), c2),
                  pl.BlockSpec((1, g), c2)],
        out_specs=(pl.BlockSpec((S_MIX, D_MODEL), lambda i: (first + i, 0)),
                   pl.BlockSpec((S_MIX, wa), row),
                   pl.BlockSpec((S_MIX, wb), row),
                   pl.BlockSpec((S_MIX, wd), row),
                   pl.BlockSpec((S_MIX, g), row)),
        compiler_params=_params(1),
        name="mix_sample",
    )(y_all, h, sa, sb, sd, ca, pw, ps, lg, lb, w0, b0, cd, dg, db)


def _attn_prompt_body(o_all_ref, q_ref, k_ref, v_ref, o_ref):
    del o_all_ref
    for hh in range(N_HEADS_X):
        sl = slice(hh * D_HEAD_X, (hh + 1) * D_HEAD_X)
        kh = k_ref[:, sl].astype(BF16)
        vh = v_ref[:, sl].astype(BF16)
        s = lax.dot_general(q_ref[:, sl], kh, (((1,), (1,)), ((), ())), preferred_element_type=F32) * ATTN_SCALE
        p = jnp.exp(s - jnp.max(s, axis=-1, keepdims=True))
        p = p / jnp.sum(p, axis=-1, keepdims=True)
        o_ref[:, sl] = jnp.dot(p.astype(BF16), vh, preferred_element_type=F32).astype(BF16)


def _attn_prompt(o_all, q, mk, mv):
    n_q = SEQ // TQ
    kv = pl.BlockSpec((N_MEM, D_MODEL), lambda b, i: (b, 0))
    qo = pl.BlockSpec((TQ, D_MODEL), lambda b, i: (b * n_q + i, 0))
    return pl.pallas_call(
        _attn_prompt_body,
        out_shape=jax.ShapeDtypeStruct((N_TOK, D_MODEL), BF16),
        grid=(BATCH, n_q),
        input_output_aliases={0: 0},
        in_specs=[pl.BlockSpec(memory_space=pl.ANY), qo, kv, kv],
        out_specs=qo,
        compiler_params=_params(2),
        name="attn_prompt",
    )(o_all, q, mk, mv)


def _paste_rows_body(all_ref, rows_ref, o_ref):
    del all_ref
    o_ref[...] = rows_ref[...]


def _paste_rows(x_all, rows):
    n, d = rows.shape
    return pl.pallas_call(
        _paste_rows_body,
        out_shape=jax.ShapeDtypeStruct(x_all.shape, x_all.dtype),
        grid=(1,),
        input_output_aliases={0: 0},
        in_specs=[pl.BlockSpec(memory_space=pl.ANY), pl.BlockSpec((n, d), lambda i: (0, 0))],
        out_specs=pl.BlockSpec((n, d), lambda i: (N_PROMPT // n, 0)),
        compiler_params=_params(1),
        name="paste_rows",
    )(x_all, rows)


def _attn_sample_body(q_ref, k_ref, v_ref, o_ref):
    for j in range(BS_ATT):
        s = jnp.sum(k_ref[j] * q_ref[j][None], axis=-1, keepdims=True) * ATTN_SCALE
        p = jnp.exp(s - jnp.max(s, axis=0, keepdims=True))
        p = p / jnp.sum(p, axis=0, keepdims=True)
        o_ref[j] = jnp.sum(p * v_ref[j], axis=0)


def _attn_sample(q, ck, cv, layer):
    qo = pl.BlockSpec((BS_ATT, N_HEADS_X, D_HEAD_X), lambda i: (i, 0, 0))
    kv = pl.BlockSpec((None, BS_ATT, N_MEM, N_HEADS_X, D_HEAD_X), lambda i: (layer, i, 0, 0, 0))
    return pl.pallas_call(
        _attn_sample_body,
        out_shape=jax.ShapeDtypeStruct((DEC_BATCH, N_HEADS_X, D_HEAD_X), F32),
        grid=(DEC_BATCH // BS_ATT,),
        in_specs=[qo, kv, kv],
        out_specs=qo,
        compiler_params=_params(1),
        name="attn_sample",
    )(q, ck, cv)


def _weights_changed(te_ref, t):
    return jnp.logical_or(t == 0, te_ref[t] != te_ref[jnp.maximum(t - 1, 0)])


def _ffn_up_body(te_ref, tv_ref, x_ref, wg_ref, wu_ref, o_ref, wgb_ref, wub_ref):
    t = pl.program_id(1)

    @pl.when(_weights_changed(te_ref, t))
    def _():
        wgb_ref[...] = wg_ref[...].astype(BF16)
        wub_ref[...] = wu_ref[...].astype(BF16)

    @pl.when(tv_ref[t] != 0)
    def _():
        x = x_ref[...]
        gate = jnp.dot(x, wgb_ref[...], preferred_element_type=F32)
        up = jnp.dot(x, wub_ref[...], preferred_element_type=F32)
        o_ref[...] = (gate * _sigmoid(gate) * up).astype(BF16)

    @pl.when(tv_ref[t] == 0)
    def _():
        o_ref[...] = jnp.zeros(o_ref.shape, BF16)


def _ffn_up(x, wg, wu, tile_expert, tile_valid, tm):
    n_tiles = tile_expert.shape[0]
    k = x.shape[1]
    grid_spec = pltpu.PrefetchScalarGridSpec(
        num_scalar_prefetch=2,
        grid=(D_FF // TF, n_tiles),
        in_specs=[pl.BlockSpec((tm, k), lambda f, t, te, tv: (t, 0)),
                  pl.BlockSpec((None, k, TF), lambda f, t, te, tv: (te[t], 0, f)),
                  pl.BlockSpec((None, k, TF), lambda f, t, te, tv: (te[t], 0, f))],
        out_specs=pl.BlockSpec((tm, TF), lambda f, t, te, tv: (t, f)),
        scratch_shapes=[pltpu.VMEM((k, TF), BF16), pltpu.VMEM((k, TF), BF16)])
    return pl.pallas_call(
        _ffn_up_body,
        out_shape=jax.ShapeDtypeStruct((n_tiles * tm, D_FF), BF16),
        grid_spec=grid_spec,
        compiler_params=_params(2),
        name="ffn_up",
    )(tile_expert, tile_valid, x, wg, wu)


def _ffn_down_body(te_ref, tv_ref, h_ref, wd_ref, o_ref, wdb_ref):
    t = pl.program_id(1)

    @pl.when(_weights_changed(te_ref, t))
    def _():
        wdb_ref[...] = wd_ref[...].astype(BF16)

    @pl.when(tv_ref[t] != 0)
    def _():
        o_ref[...] = jnp.dot(h_ref[...], wdb_ref[...], preferred_element_type=F32)

    @pl.when(tv_ref[t] == 0)
    def _():
        o_ref[...] = jnp.zeros(o_ref.shape, F32)


def _ffn_down(hmid, wd, tile_expert, tile_valid, tm):
    n_tiles = tile_expert.shape[0]
    n = wd.shape[2]
    grid_spec = pltpu.PrefetchScalarGridSpec(
        num_scalar_prefetch=2,
        grid=(n // TN_DOWN, n_tiles),
        in_specs=[pl.BlockSpec((tm, D_FF), lambda j, t, te, tv: (t, 0)),
                  pl.BlockSpec((None, D_FF, TN_DOWN), lambda j, t, te, tv: (te[t], 0, j))],
        out_specs=pl.BlockSpec((tm, TN_DOWN), lambda j, t, te, tv: (t, j)),
        scratch_shapes=[pltpu.VMEM((D_FF, TN_DOWN), BF16)])
    return pl.pallas_call(
        _ffn_down_body,
        out_shape=jax.ShapeDtypeStruct((n_tiles * tm, n), F32),
        grid_spec=grid_spec,
        compiler_params=_params(2),
        name="ffn_down",
    )(tile_expert, tile_valid, hmid, wd)


def _router_body(x_ref, r_ref, idx_ref, gate_ref):
    logits = jnp.dot(x_ref[...], r_ref[...], preferred_element_type=F32, precision=lax.Precision.HIGHEST)
    lane = lax.broadcasted_iota(jnp.int32, logits.shape, 1).astype(F32)
    n = float(N_EXPERTS)
    m1 = jnp.max(logits, axis=-1, keepdims=True)
    i1 = jnp.min(jnp.where(logits == m1, lane, n), axis=-1, keepdims=True)
    rest = jnp.where(lane == i1, -jnp.inf, logits)
    m2 = jnp.max(rest, axis=-1, keepdims=True)
    i2 = jnp.min(jnp.where(rest == m2, lane, n), axis=-1, keepdims=True)
    e2 = jnp.exp(m2 - m1)
    g1 = 1.0 / (1.0 + e2)
    first = lax.broadcasted_iota(jnp.int32, idx_ref.shape, 1) == 0
    idx_ref[...] = jnp.where(first, i1, i2).astype(jnp.int32)
    gate_ref[...] = jnp.where(first, g1, e2 * g1)


def _router(x, r):
    m, k = x.shape
    out = pl.BlockSpec((TM, TOP_K), lambda i: (i, 0))
    return pl.pallas_call(
        _router_body,
        out_shape=(jax.ShapeDtypeStruct((m, TOP_K), jnp.int32), jax.ShapeDtypeStruct((m, TOP_K), F32)),
        grid=(m // TM,),
        in_specs=[pl.BlockSpec((TM, k), lambda i: (i, 0)),
                  pl.BlockSpec((k, N_EXPERTS), lambda i: (0, 0))],
        out_specs=(out, out),
        compiler_params=_params(1),
        name="router",
    )(x, r)


def _route_tables(idx, expert_base):
    a = idx.reshape(-1)
    onehot = (a[:, None] == jnp.arange(N_EXPERTS, dtype=jnp.int32)[None, :]).astype(jnp.int32)
    csum = jnp.cumsum(onehot, axis=0)
    rank = jnp.sum(onehot * csum, axis=1) - 1
    counts = csum[-1]
    padded = (counts + (TM_E - 1)) // TM_E * TM_E
    ends = jnp.cumsum(padded)
    row = jnp.sum(onehot * (ends - padded)[None, :], axis=1) + rank
    row_token = jnp.zeros((N_ROWS_E,), jnp.int32).at[row].set(jnp.arange(N_ASSIGN, dtype=jnp.int32) // TOP_K)
    tile_start = jnp.arange(N_TILES_E, dtype=jnp.int32) * TM_E
    tile_expert = jnp.sum((tile_start[:, None] >= ends[None, :]).astype(jnp.int32), axis=1)
    tile_expert = jnp.minimum(tile_expert, N_EXPERTS - 1) + expert_base
    tile_valid = (tile_start < ends[-1]).astype(jnp.int32)
    return row, row_token, tile_expert, tile_valid


def _start_row_copies(src_hbm, rows_ref, first, stride, dst, sem, n):
    def body(r, carry):
        src_row = rows_ref[first + stride * r]
        pltpu.make_async_copy(src_hbm.at[pl.ds(src_row, 1)], dst.at[pl.ds(r, 1)], sem).start()
        return carry

    lax.fori_loop(0, n, body, 0)


def _wait_row_copies(src_hbm, dst, sem, n):
    pltpu.make_async_copy(src_hbm.at[pl.ds(0, n)], dst, sem).wait()


def _gather_rows_body(tok_ref, x_hbm, o_ref, buf, sem):
    t = pl.program_id(0)
    n_tiles = pl.num_programs(0)

    @pl.when(t == 0)
    def _():
        _start_row_copies(x_hbm, tok_ref, 0, 1, buf.at[0], sem.at[0], TM_E)

    @pl.when(t + 1 < n_tiles)
    def _():
        nxt = (t + 1) % 2
        _start_row_copies(x_hbm, tok_ref, (t + 1) * TM_E, 1, buf.at[nxt], sem.at[nxt], TM_E)

    slot = t % 2
    _wait_row_copies(x_hbm, buf.at[slot], sem.at[slot], TM_E)
    for c in range(buf.shape[2]):
        o_ref[:, c * LANES:(c + 1) * LANES] = buf[slot, :, c, :].astype(BF16)


def _gather_rows(x, row_token):
    _, nc, lanes = x.shape
    d = nc * lanes
    grid_spec = pltpu.PrefetchScalarGridSpec(
        num_scalar_prefetch=1,
        grid=(N_TILES_E,),
        in_specs=[pl.BlockSpec(memory_space=pl.ANY)],
        out_specs=pl.BlockSpec((TM_E, d), lambda t, tok: (t, 0)),
        scratch_shapes=[pltpu.VMEM((2, TM_E, nc, lanes), F32), pltpu.SemaphoreType.DMA((2,))])
    return pl.pallas_call(
        _gather_rows_body,
        out_shape=jax.ShapeDtypeStruct((N_ROWS_E, d), BF16),
        grid_spec=grid_spec,
        compiler_params=_params(1),
        name="gather_rows",
    )(row_token, x)


def _combine_body(row_ref, ys_hbm, gate_ref, res_ref, g_ref, b_ref, o_ref, ob_ref, buf, sem):
    t = pl.program_id(0)
    n_tiles = pl.num_programs(0)
    tm = TM_COMB

    def start(tile, slot):
        for k in range(TOP_K):
            _start_row_copies(ys_hbm, row_ref, TOP_K * tile * tm + k, TOP_K, buf.at[slot, k], sem.at[slot], tm)

    @pl.when(t == 0)
    def _():
        start(0, 0)

    @pl.when(t + 1 < n_tiles)
    def _():
        start(t + 1, (t + 1) % 2)

    slot = t % 2
    for k in range(TOP_K):
        _wait_row_copies(ys_hbm, buf.at[slot, k], sem.at[slot], tm)
    y = gate_ref[:, 0:1] * buf[slot, 0] + gate_ref[:, 1:2] * buf[slot, 1]
    _residual_ln_store(y, res_ref, g_ref, b_ref, o_ref, ob_ref)


def _combine(ys, row, gate, res, g, b):
    m, n = res.shape
    tm = TM_COMB
    rows = lambda i, r: (i, 0)
    fixed = lambda i, r: (0, 0)
    grid_spec = pltpu.PrefetchScalarGridSpec(
        num_scalar_prefetch=1,
        grid=(m // tm,),
        in_specs=[pl.BlockSpec(memory_space=pl.ANY),
                  pl.BlockSpec((tm, TOP_K), rows),
                  pl.BlockSpec((tm, n), rows),
                  pl.BlockSpec((1, n), fixed),
                  pl.BlockSpec((1, n), fixed)],
        out_specs=(pl.BlockSpec((tm, n), rows), pl.BlockSpec((tm, n), rows)),
        scratch_shapes=[pltpu.VMEM((2, TOP_K, tm, n), F32), pltpu.SemaphoreType.DMA((2,))])
    return pl.pallas_call(
        _combine_body,
        out_shape=(jax.ShapeDtypeStruct((m, n), F32), jax.ShapeDtypeStruct((m, n), BF16)),
        grid_spec=grid_spec,
        compiler_params=_params(1),
        name="combine",
    )(row, ys, gate, res, g, b)


def kernel(x_prompt, x_sample, state_a, state_b, state_d, cache_mem_k, cache_mem_v, mem_prompt, w_in, conv_a, pool_w, pool_scale, sg_ln_g, sg_ln_b, sg_w, sg_b, conv_d, cd_ln_g, cd_ln_b, w_out, w_q, w_k, w_v, w_o, ln_g, ln_b, dense_w_gate, dense_w_up, dense_w_down, moe_router, moe_w_gate, moe_w_up, moe_w_down):
    d, g = D_MODEL, D_GRP
    x = jnp.concatenate([x_prompt.reshape(N_PROMPT, d), x_sample.reshape(DEC_BATCH, d)], axis=0)
    xb = x.astype(BF16)
    mem_b = mem_prompt.reshape(BATCH * N_MEM, d).astype(BF16)
    n_tiles = N_TOK // TM
    all_valid = jnp.ones((n_tiles,), jnp.int32)
    moe_wg = moe_w_gate.reshape(-1, d, D_FF)
    moe_wu = moe_w_up.reshape(-1, d, D_FF)
    moe_wd = moe_w_down.reshape(-1, D_FF, d)
    vec = lambda v: v.reshape(1, -1)

    sa_p, sb_p, sd_p, mk_p, mv_p, sa_s, sb_s, sd_s, sc_s = [], [], [], [], [], [], [], [], []
    for l in range(DEPTH):
        h = _matmul(xb, w_in, l, F32, TM, TN, "w_in")
        y_p, ha, hb, hd = _mix_prompt(jnp.zeros((N_TOK, d), BF16), h, conv_a[l], pool_w[l], vec(pool_scale[l]), vec(sg_ln_g[l]), vec(sg_ln_b[l]),
                                      sg_w[l], sg_b[l].T, conv_d[l], vec(cd_ln_g[l]), vec(cd_ln_b[l]))
        y_mix, ta, tb, td, tc = _mix_sample(
            y_p, h, state_a[l].reshape(DEC_BATCH, -1), state_b[l].reshape(DEC_BATCH, -1), state_d[l].reshape(DEC_BATCH, -1),
            conv_a[l], pool_w[l], vec(pool_scale[l]), vec(sg_ln_g[l]), vec(sg_ln_b[l]),
            vec(jnp.repeat(sg_w[l, :, 0, 0], CHUNK)), vec(jnp.repeat(sg_b[l, :, 0], CHUNK)),
            conv_d[l], vec(cd_ln_g[l]), vec(cd_ln_b[l]))
        x, xb = _mm_ln_resident(y_mix, w_out, l, x, vec(ln_g[l, 0]), vec(ln_b[l, 0]), "w_out_ln")

        q = _matmul(xb, w_q, l, BF16, TM, TN, "w_q")
        mk = _matmul(mem_b, w_k, l, F32, 512, TN, "w_k")
        mv = _matmul(mem_b, w_v, l, F32, 512, TN, "w_v")
        o_p = _attn_prompt(jnp.zeros((N_TOK, d), BF16), q, mk, mv)
        o_s = _attn_sample(q[N_PROMPT:].astype(F32).reshape(DEC_BATCH, N_HEADS_X, D_HEAD_X), cache_mem_k, cache_mem_v, l)
        o = _paste_rows(o_p, o_s.reshape(DEC_BATCH, d).astype(BF16))
        j = l // 2
        is_moe = l % 2 == 1
        x, xb, *x_rows3d = _mm_ln_resident(o, w_o, l, x, vec(ln_g[l, 1]), vec(ln_b[l, 1]), "w_o_ln", with_rows3d=is_moe)

        if not is_moe:
            dense_tiles = (jnp.full((n_tiles,), j, jnp.int32), all_valid)
            hmid = _ffn_up(xb, dense_w_gate, dense_w_up, *dense_tiles, TM)
            y = _ffn_down(hmid, dense_w_down, *dense_tiles, TM)
            x, xb = _add_ln(y, x, vec(ln_g[l, 2]), vec(ln_b[l, 2]))
        else:
            idx, gate = _router(x, moe_router[j])
            row, row_token, tile_expert, tile_valid = _route_tables(idx, j * N_EXPERTS)
            xs = _gather_rows(x_rows3d[0], row_token)
            hmid = _ffn_up(xs, moe_wg, moe_wu, tile_expert, tile_valid, TM_E)
            ys = _ffn_down(hmid, moe_wd, tile_expert, tile_valid, TM_E)
            x, xb = _combine(ys, row, gate, x, vec(ln_g[l, 2]), vec(ln_b[l, 2]))

        sa_p.append(ha); sb_p.append(hb); sd_p.append(hd)
        mk_p.append(mk.reshape(BATCH, N_MEM, N_HEADS_X, D_HEAD_X)); mv_p.append(mv.reshape(BATCH, N_MEM, N_HEADS_X, D_HEAD_X))
        sa_s.append(ta.reshape(DEC_BATCH, CONV_A - 1, g)); sb_s.append(tb.reshape(DEC_BATCH, POOL_HIST, g))
        sd_s.append(td.reshape(DEC_BATCH, CONV_D - 1, g)); sc_s.append(tc.reshape(DEC_BATCH, 1, g))

    return (x[:N_PROMPT].reshape(BATCH, SEQ, d), x[N_PROMPT:].reshape(DEC_BATCH, 1, d),
            jnp.stack(sa_p), jnp.stack(sb_p), jnp.stack(sd_p), jnp.stack(mk_p), jnp.stack(mv_p),
            jnp.stack(sa_s), jnp.stack(sb_s), jnp.stack(sd_s), jnp.stack(sc_s))
```

```python
import functools

import jax
import jax.numpy as jnp
from jax import lax
from jax.experimental import pallas as pl
from jax.experimental.pallas import tpu as pltpu

F32 = jnp.float32
BF16 = jnp.bfloat16

D_MODEL = 2048
BATCH = 4
SEQ = 2048
DEPTH = 4
DEC_BATCH = 128
PAST_LEN = 16384
D_GRP = 512
D_IN = 8 * D_GRP
CONV_A = 3
POOL_WINDOWS = (2, 4, 8, 16)
D_POOL = 128
POOL_HIST = 15
CHUNK = 128
N_HEADS_C = 4
CONV_D = 31
N_MEM = 256
N_HEADS_X = 4
D_HEAD_X = 512
D_FF = 5632
N_EXPERTS = 8
ALPHA = (2 * DEPTH) ** 0.25
LN_EPS = 1e-5
ATTN_SCALE = D_HEAD_X ** -0.5

N_PROMPT = BATCH * SEQ
N_TOK = N_PROMPT + DEC_BATCH

VMEM_LIMIT_BYTES = 52 * 1024 * 1024

TM = 640
TM_LN = 320
TN = 1024
TF = 512
LN_ROW_GROUPS = 2
LANES = 128
TN_DOWN = 512
T_MIX = 256
SUBLANES = 8
ROWS_D = 64
LANES_D = 256
ROWS_LN = 32
HALO_A, HALO_B, HALO_D = 8, 16, 32
S_MIX = 32
TQ = 512
BS_ATT = 4
TM_E = 512
TM_COMB = 320
TOP_K = 2
N_DMA_QUEUES = 2
N_ASSIGN = TOP_K * N_TOK
N_TILES_E = -(-(N_ASSIGN + N_EXPERTS * (TM_E - 1)) // TM_E)
N_ROWS_E = N_TILES_E * TM_E


def _params(n_axes):
    return pltpu.CompilerParams(dimension_semantics=("arbitrary",) * n_axes,
                                vmem_limit_bytes=VMEM_LIMIT_BYTES)


def _layer_norm(x, g, b):
    mu = jnp.mean(x, axis=-1, keepdims=True)
    xc = x - mu
    var = jnp.mean(xc * xc, axis=-1, keepdims=True)
    return xc * lax.rsqrt(var + LN_EPS) * g + b


def _sigmoid(x):
    return 1.0 / (1.0 + jnp.exp(-x))


def _mm_body(x_ref, w_ref, o_ref, wb_ref):
    @pl.when(pl.program_id(1) == 0)
    def _():
        wb_ref[...] = w_ref[...].astype(BF16)

    o_ref[...] = jnp.dot(x_ref[...], wb_ref[...], preferred_element_type=F32).astype(o_ref.dtype)


def _matmul(x, w, layer, out_dtype, tm, tn, name):
    m, k = x.shape
    n = w.shape[2]
    return pl.pallas_call(
        _mm_body,
        out_shape=jax.ShapeDtypeStruct((m, n), out_dtype),
        grid=(n // tn, m // tm),
        in_specs=[pl.BlockSpec((tm, k), lambda j, i: (i, 0)),
                  pl.BlockSpec((None, k, tn), lambda j, i: (layer, 0, j))],
        out_specs=pl.BlockSpec((tm, tn), lambda j, i: (i, j)),
        scratch_shapes=[pltpu.VMEM((k, tn), BF16)],
        compiler_params=_params(2),
        name=name,
    )(x, w)


def _residual_ln_store(acc, res_ref, g_ref, b_ref, o_ref, ob_ref):
    y = _layer_norm(ALPHA * res_ref[...] + acc, g_ref[...], b_ref[...])
    o_ref[...] = y
    ob_ref[...] = y.astype(BF16)


def _mm_ln_resident_body(a_ref, w_ref, res_ref, g_ref, b_ref, o_ref, ob_ref, wb_ref):
    @pl.when(pl.program_id(0) == 0)
    def _():
        wb_ref[...] = w_ref[...].astype(BF16)

    rows_per_group = a_ref.shape[0] // LN_ROW_GROUPS
    for s in range(LN_ROW_GROUPS):
        rows = slice(s * rows_per_group, (s + 1) * rows_per_group)
        acc = jnp.dot(a_ref[rows, :], wb_ref[...], preferred_element_type=F32)
        y = _layer_norm(ALPHA * res_ref[rows, :] + acc, g_ref[...], b_ref[...])
        o_ref[rows, :] = y
        ob_ref[rows, :] = y.astype(BF16)


def _mm_ln_resident(a, w, layer, res, g, b, name):
    m, k = a.shape
    n = w.shape[2]
    tm = TM_LN
    row = lambda i: (i, 0)
    fixed = lambda i: (0, 0)
    return pl.pallas_call(
        _mm_ln_resident_body,
        out_shape=(jax.ShapeDtypeStruct((m, n), F32), jax.ShapeDtypeStruct((m, n), BF16)),
        grid=(m // tm,),
        in_specs=[pl.BlockSpec((tm, k), row),
                  pl.BlockSpec((None, k, n), lambda i: (layer, 0, 0), pipeline_mode=pl.Buffered(1)),
                  pl.BlockSpec((tm, n), row),
                  pl.BlockSpec((1, n), fixed),
                  pl.BlockSpec((1, n), fixed)],
        out_specs=(pl.BlockSpec((tm, n), row), pl.BlockSpec((tm, n), row)),
        scratch_shapes=[pltpu.VMEM((k, n), BF16)],
        compiler_params=_params(1),
        name=name,
    )(a, w, res, g, b)


def _add_ln_body(y_ref, res_ref, g_ref, b_ref, o_ref, ob_ref):
    _residual_ln_store(y_ref[...], res_ref, g_ref, b_ref, o_ref, ob_ref)


def _add_ln(y, res, g, b):
    m, n = res.shape
    tm = TM_LN
    row = lambda i: (i, 0)
    fixed = lambda i: (0, 0)
    return pl.pallas_call(
        _add_ln_body,
        out_shape=(jax.ShapeDtypeStruct((m, n), F32), jax.ShapeDtypeStruct((m, n), BF16)),
        grid=(m // tm,),
        in_specs=[pl.BlockSpec((tm, n), row), pl.BlockSpec((tm, n), row),
                  pl.BlockSpec((1, n), fixed), pl.BlockSpec((1, n), fixed)],
        out_specs=(pl.BlockSpec((tm, n), row), pl.BlockSpec((tm, n), row)),
        compiler_params=_params(1),
        name="add_ln",
    )(y, res, g, b)


def _mix_prompt_body(y_all_ref, h_ref, ca_ref, pw_ref, ps_ref, lg_ref, lb_ref, sw_ref, sbt_ref, cd_ref, dg_ref, db_ref,
                     y_ref, sa_ref, sb_ref, sd_ref, ea_ref, eb_ref, ed_ref, vn_ref, cv_ref, *, n_steps):
    del y_all_ref
    t = T_MIX
    g = D_GRP
    s = pl.program_id(1)

    @pl.when(s == 0)
    def _():
        ea_ref[0:HALO_A, :] = jnp.zeros((HALO_A, g), F32)
        eb_ref[0:HALO_B, :] = jnp.zeros((HALO_B, g), F32)
        ed_ref[0:HALO_D, :] = jnp.zeros((HALO_D, g), F32)

    e = h_ref[:, g:2 * g] * h_ref[:, 2 * g:3 * g]
    ea_ref[HALO_A:HALO_A + t, :] = e
    conv_a = (ca_ref[2:3, :] * e + ca_ref[1:2, :] * ea_ref[HALO_A - 1:HALO_A - 1 + t, :]
              + ca_ref[0:1, :] * ea_ref[HALO_A - 2:HALO_A - 2 + t, :])
    y_ref[:, 0:g] = (h_ref[:, 0:g] * conv_a).astype(BF16)

    eb_ref[HALO_B:HALO_B + t, :] = h_ref[:, 3 * g:4 * g]
    pos = s * t + lax.broadcasted_iota(jnp.int32, (t, 1), 0)
    for gi, w in enumerate(POOL_WINDOWS):
        lo = gi * D_POOL
        cur = h_ref[:, 3 * g + lo:3 * g + lo + D_POOL]
        win = cur
        for k in range(1, w):
            win = win + eb_ref[HALO_B - k:HALO_B - k + t, lo:lo + D_POOL]
        cnt = jnp.minimum(pos + 1, w).astype(F32)
        pooled = win / cnt - cur
        yb = jnp.dot(pooled.astype(BF16), pw_ref[gi].astype(BF16), preferred_element_type=F32)
        y_ref[:, g + lo:g + lo + D_POOL] = (yb * ps_ref[:, lo:lo + D_POOL]).astype(BF16)

    vn_ref[...] = _layer_norm(h_ref[:, 5 * g:6 * g], lg_ref[...], lb_ref[...])
    n_chunks = t // CHUNK
    rows = lax.broadcasted_iota(jnp.int32, (CHUNK, CHUNK), 0)
    cols = lax.broadcasted_iota(jnp.int32, (CHUNK, CHUNK), 1)
    for hh in range(N_HEADS_C):
        lo = hh * CHUNK
        w_tril = jnp.where(rows >= cols, sw_ref[hh], 0.0).astype(BF16)
        rhs = jnp.concatenate([vn_ref[c * CHUNK:(c + 1) * CHUNK, lo:lo + CHUNK] for c in range(n_chunks)], axis=1)
        mixed = jnp.dot(w_tril, rhs.astype(BF16), preferred_element_type=F32)
        bias = sbt_ref[:, hh:hh + 1]
        for c in range(n_chunks):
            u = h_ref[c * CHUNK:(c + 1) * CHUNK, 4 * g + lo:4 * g + lo + CHUNK]
            y_ref[c * CHUNK:(c + 1) * CHUNK, 2 * g + lo:2 * g + lo + CHUNK] = (
                u * (mixed[:, c * CHUNK:(c + 1) * CHUNK] + bias)).astype(BF16)

    ed_ref[HALO_D:HALO_D + t, :] = h_ref[:, 6 * g:7 * g] * _sigmoid(h_ref[:, 7 * g:8 * g])
    for rb in range(t // ROWS_D):
        r0 = HALO_D + rb * ROWS_D
        for lb in range(g // LANES_D):
            ls = slice(lb * LANES_D, (lb + 1) * LANES_D)
            acc = None
            for b in range(SUBLANES):
                z_b = None
                for a in range(-(-CONV_D // SUBLANES)):
                    shift = SUBLANES * a + b
                    if shift < CONV_D:
                        lo = r0 - SUBLANES * (a + 1)
                        term = cd_ref[CONV_D - 1 - shift:CONV_D - shift, ls] * ed_ref[lo:lo + ROWS_D + SUBLANES, ls]
                        z_b = term if z_b is None else z_b + term
                part = z_b[SUBLANES - b:SUBLANES - b + ROWS_D, :]
                acc = part if acc is None else acc + part
            cv_ref[rb * ROWS_D:(rb + 1) * ROWS_D, ls] = acc
    for rb in range(t // ROWS_LN):
        rows = slice(rb * ROWS_LN, (rb + 1) * ROWS_LN)
        z = _layer_norm(cv_ref[rows, :], dg_ref[...], db_ref[...])
        y_ref[rows, 3 * g:4 * g] = (z * _sigmoid(z)).astype(BF16)

    @pl.when(s == n_steps - 1)
    def _():
        sa_ref[0] = ea_ref[HALO_A + t - (CONV_A - 1):HALO_A + t, :]
        sb_ref[0] = eb_ref[HALO_B + t - POOL_HIST:HALO_B + t, :]
        sd_ref[0] = ed_ref[HALO_D + t - (CONV_D - 1):HALO_D + t, :]

    ea_ref[0:HALO_A, :] = ea_ref[t:t + HALO_A, :]
    eb_ref[0:HALO_B, :] = eb_ref[t:t + HALO_B, :]
    ed_ref[0:HALO_D, :] = ed_ref[t:t + HALO_D, :]


def _mix_prompt(y_all, h, ca, pw, ps, lg, lb, sw, sbt, cd, dg, db):
    n_steps = SEQ // T_MIX
    g = D_GRP
    c2 = lambda b, s: (0, 0)
    c3 = lambda b, s: (0, 0, 0)
    st = lambda b, s: (b, 0, 0)
    return pl.pallas_call(
        functools.partial(_mix_prompt_body, n_steps=n_steps),
        out_shape=(jax.ShapeDtypeStruct((N_TOK, D_MODEL), BF16),
                   jax.ShapeDtypeStruct((BATCH, CONV_A - 1, g), F32),
                   jax.ShapeDtypeStruct((BATCH, POOL_HIST, g), F32),
                   jax.ShapeDtypeStruct((BATCH, CONV_D - 1, g), F32)),
        grid=(BATCH, n_steps),
        input_output_aliases={0: 0},
        in_specs=[pl.BlockSpec(memory_space=pl.ANY),
                  pl.BlockSpec((T_MIX, D_IN), lambda b, s: (b * n_steps + s, 0)),
                  pl.BlockSpec((CONV_A, g), c2),
                  pl.BlockSpec((len(POOL_WINDOWS), D_POOL, D_POOL), c3),
                  pl.BlockSpec((1, g), c2),
                  pl.BlockSpec((1, g), c2),
                  pl.BlockSpec((1, g), c2),
                  pl.BlockSpec((N_HEADS_C, CHUNK, CHUNK), c3),
                  pl.BlockSpec((CHUNK, N_HEADS_C), c2),
                  pl.BlockSpec((CONV_D, g), c2),
                  pl.BlockSpec((1, g), c2),
                  pl.BlockSpec((1, g), c2)],
        out_specs=(pl.BlockSpec((T_MIX, D_MODEL), lambda b, s: (b * n_steps + s, 0)),
                   pl.BlockSpec((1, CONV_A - 1, g), st),
                   pl.BlockSpec((1, POOL_HIST, g), st),
                   pl.BlockSpec((1, CONV_D - 1, g), st)),
        scratch_shapes=[pltpu.VMEM((HALO_A + T_MIX, g), F32),
                        pltpu.VMEM((HALO_B + T_MIX, g), F32),
                        pltpu.VMEM((HALO_D + T_MIX, g), F32),
                        pltpu.VMEM((T_MIX, g), F32),
                        pltpu.VMEM((T_MIX, g), F32)],
        compiler_params=_params(2),
        name="mix_prompt",
    )(y_all, h, ca, pw, ps, lg, lb, sw, sbt, cd, dg, db)


def _mix_sample_body(y_all_ref, h_ref, sa_ref, sb_ref, sd_ref, ca_ref, pw_ref, ps_ref, lg_ref, lb_ref, w0_ref, b0_ref,
                     cd_ref, dg_ref, db_ref, y_ref, na_ref, nb_ref, nd_ref, vn_ref):
    del y_all_ref
    g = D_GRP

    e = h_ref[:, g:2 * g] * h_ref[:, 2 * g:3 * g]
    conv_a = ca_ref[2:3, :] * e + ca_ref[1:2, :] * sa_ref[:, g:2 * g] + ca_ref[0:1, :] * sa_ref[:, 0:g]
    y_ref[:, 0:g] = (h_ref[:, 0:g] * conv_a).astype(BF16)
    na_ref[:, 0:g] = sa_ref[:, g:2 * g]
    na_ref[:, g:2 * g] = e

    for gi, w in enumerate(POOL_WINDOWS):
        lo = gi * D_POOL
        cur = h_ref[:, 3 * g + lo:3 * g + lo + D_POOL]
        win = cur
        for k in range(1, w):
            col = (POOL_HIST - k) * g + lo
            win = win + sb_ref[:, col:col + D_POOL]
        cnt = float(min(PAST_LEN + 1, w))
        pooled = win / cnt - cur
        yb = jnp.dot(pooled.astype(BF16), pw_ref[gi].astype(BF16), preferred_element_type=F32)
        y_ref[:, g + lo:g + lo + D_POOL] = (yb * ps_ref[:, lo:lo + D_POOL]).astype(BF16)
    nb_ref[:, 0:(POOL_HIST - 1) * g] = sb_ref[:, g:POOL_HIST * g]
    nb_ref[:, (POOL_HIST - 1) * g:POOL_HIST * g] = h_ref[:, 3 * g:4 * g]

    vn = _layer_norm(h_ref[:, 5 * g:6 * g], lg_ref[...], lb_ref[...])
    vn_ref[...] = vn
    y_ref[:, 2 * g:3 * g] = (h_ref[:, 4 * g:5 * g] * (w0_ref[...] * vn + b0_ref[...])).astype(BF16)

    glu = h_ref[:, 6 * g:7 * g] * _sigmoid(h_ref[:, 7 * g:8 * g])
    acc = cd_ref[CONV_D - 1:CONV_D, :] * glu
    for k in range(CONV_D - 1):
        acc = acc + cd_ref[k:k + 1, :] * sd_ref[:, k * g:(k + 1) * g]
    z = _layer_norm(acc, dg_ref[...], db_ref[...])
    y_ref[:, 3 * g:4 * g] = (z * _sigmoid(z)).astype(BF16)
    nd_ref[:, 0:(CONV_D - 2) * g] = sd_ref[:, g:(CONV_D - 1) * g]
    nd_ref[:, (CONV_D - 2) * g:(CONV_D - 1) * g] = glu


def _mix_sample(y_all, h, sa, sb, sd, ca, pw, ps, lg, lb, w0, b0, cd, dg, db):
    g = D_GRP
    n_steps = DEC_BATCH // S_MIX
    first = N_PROMPT // S_MIX
    row = lambda i: (i, 0)
    c2 = lambda i: (0, 0)
    c3 = lambda i: (0, 0, 0)
    wa, wb, wd = (CONV_A - 1) * g, POOL_HIST * g, (CONV_D - 1) * g
    return pl.pallas_call(
        _mix_sample_body,
        out_shape=(jax.ShapeDtypeStruct((N_TOK, D_MODEL), BF16),
                   jax.ShapeDtypeStruct((DEC_BATCH, wa), F32),
                   jax.ShapeDtypeStruct((DEC_BATCH, wb), F32),
                   jax.ShapeDtypeStruct((DEC_BATCH, wd), F32),
                   jax.ShapeDtypeStruct((DEC_BATCH, g), F32)),
        grid=(n_steps,),
        input_output_aliases={0: 0},
        in_specs=[pl.BlockSpec(memory_space=pl.ANY),
                  pl.BlockSpec((S_MIX, D_IN), lambda i: (first + i, 0)),
                  pl.BlockSpec((S_MIX, wa), row),
                  pl.BlockSpec((S_MIX, wb), row),
                  pl.BlockSpec((S_MIX, wd), row),
                  pl.BlockSpec((CONV_A, g), c2),
                  pl.BlockSpec((len(POOL_WINDOWS), D_POOL, D_POOL), c3),
                  pl.BlockSpec((1, g), c2),
                  pl.BlockSpec((1, g), c2),
                  pl.BlockSpec((1, g), c2),
                  pl.BlockSpec((1, g), c2),
                  pl.BlockSpec((1, g), c2),
                  pl.BlockSpec((CONV_D, g), c2),
                  pl.BlockSpec((1, g), c2),
                  pl.BlockSpec((1, g), c2)],
        out_specs=(pl.BlockSpec((S_MIX, D_MODEL), lambda i: (first + i, 0)),
                   pl.BlockSpec((S_MIX, wa), row),
                   pl.BlockSpec((S_MIX, wb), row),
                   pl.BlockSpec((S_MIX, wd), row),
                   pl.BlockSpec((S_MIX, g), row)),
        compiler_params=_params(1),
        name="mix_sample",
    )(y_all, h, sa, sb, sd, ca, pw, ps, lg, lb, w0, b0, cd, dg, db)


def _attn_prompt_body(o_all_ref, q_ref, k_ref, v_ref, o_ref):
    del o_all_ref
    for hh in range(N_HEADS_X):
        sl = slice(hh * D_HEAD_X, (hh + 1) * D_HEAD_X)
        kh = k_ref[:, sl].astype(BF16)
        vh = v_ref[:, sl].astype(BF16)
        s = lax.dot_general(q_ref[:, sl], kh, (((1,), (1,)), ((), ())), preferred_element_type=F32) * ATTN_SCALE
        p = jnp.exp(s - jnp.max(s, axis=-1, keepdims=True))
        p = p / jnp.sum(p, axis=-1, keepdims=True)
        o_ref[:, sl] = jnp.dot(p.astype(BF16), vh, preferred_element_type=F32).astype(BF16)


def _attn_prompt(o_all, q, mk, mv):
    n_q = SEQ // TQ
    kv = pl.BlockSpec((N_MEM, D_MODEL), lambda b, i: (b, 0))
    qo = pl.BlockSpec((TQ, D_MODEL), lambda b, i: (b * n_q + i, 0))
    return pl.pallas_call(
        _attn_prompt_body,
        out_shape=jax.ShapeDtypeStruct((N_TOK, D_MODEL), BF16),
        grid=(BATCH, n_q),
        input_output_aliases={0: 0},
        in_specs=[pl.BlockSpec(memory_space=pl.ANY), qo, kv, kv],
        out_specs=qo,
        compiler_params=_params(2),
        name="attn_prompt",
    )(o_all, q, mk, mv)


def _paste_rows_body(all_ref, rows_ref, o_ref):
    del all_ref
    o_ref[...] = rows_ref[...]


def _paste_rows(x_all, rows):
    n, d = rows.shape
    return pl.pallas_call(
        _paste_rows_body,
        out_shape=jax.ShapeDtypeStruct(x_all.shape, x_all.dtype),
        grid=(1,),
        input_output_aliases={0: 0},
        in_specs=[pl.BlockSpec(memory_space=pl.ANY), pl.BlockSpec((n, d), lambda i: (0, 0))],
        out_specs=pl.BlockSpec((n, d), lambda i: (N_PROMPT // n, 0)),
        compiler_params=_params(1),
        name="paste_rows",
    )(x_all, rows)


N_LANE_TILES_X = D_HEAD_X // LANES
HALF = N_HEADS_X


def _attn_sample_body(q_ref, *refs):
    nc = N_LANE_TILES_X
    packed = (N_MEM // 2, 2 * N_HEADS_X, LANES)
    k_refs, v_refs, o_ref = refs[:nc], refs[nc:2 * nc], refs[2 * nc]
    for j in range(BS_ATT):
        t = k_refs[0][j].reshape(packed) * q_ref[j, 0][None]
        for c in range(1, nc):
            t = t + k_refs[c][j].reshape(packed) * q_ref[j, c][None]
        s = jnp.sum(t, axis=-1, keepdims=True) * ATTN_SCALE
        m = jnp.max(s, axis=0, keepdims=True)
        m = jnp.maximum(m, pltpu.roll(m, HALF, 1))
        p = jnp.exp(s - m)
        l = jnp.sum(p, axis=0, keepdims=True)
        p = p / (l + pltpu.roll(l, HALF, 1))
        for c in range(nc):
            acc = jnp.sum(p * v_refs[c][j].reshape(packed), axis=0)
            o_ref[j, c] = acc + pltpu.roll(acc, HALF, 0)


def _attn_sample(q, ck, cv, layer):
    nc = N_LANE_TILES_X
    q2 = q.reshape(DEC_BATCH, N_HEADS_X, nc, LANES).transpose(0, 2, 1, 3)
    q2 = jnp.concatenate([q2, q2], axis=2)
    qo = pl.BlockSpec((BS_ATT, nc, 2 * N_HEADS_X, LANES), lambda i: (i, 0, 0, 0))
    kv = [pl.BlockSpec((None, BS_ATT, N_MEM, N_HEADS_X, LANES), lambda i, c=c: (layer, i, 0, 0, c)) for c in range(nc)]
    o2 = pl.pallas_call(
        _attn_sample_body,
        out_shape=jax.ShapeDtypeStruct((DEC_BATCH, nc, 2 * N_HEADS_X, LANES), F32),
        grid=(DEC_BATCH // BS_ATT,),
        in_specs=[qo] + kv + kv,
        out_specs=qo,
        compiler_params=_params(1),
        name="attn_sample",
    )(q2, *([ck] * nc), *([cv] * nc))
    return o2[:, :, :N_HEADS_X, :].transpose(0, 2, 1, 3).reshape(DEC_BATCH, D_MODEL)


def _weights_changed(te_ref, t):
    return jnp.logical_or(t == 0, te_ref[t] != te_ref[jnp.maximum(t - 1, 0)])


def _ffn_up_body(te_ref, tv_ref, x_ref, wg_ref, wu_ref, o_ref, wgb_ref, wub_ref):
    t = pl.program_id(1)

    @pl.when(_weights_changed(te_ref, t))
    def _():
        wgb_ref[...] = wg_ref[...].astype(BF16)
        wub_ref[...] = wu_ref[...].astype(BF16)

    @pl.when(tv_ref[t] != 0)
    def _():
        x = x_ref[...]
        gate = jnp.dot(x, wgb_ref[...], preferred_element_type=F32)
        up = jnp.dot(x, wub_ref[...], preferred_element_type=F32)
        o_ref[...] = (gate * _sigmoid(gate) * up).astype(BF16)

    @pl.when(tv_ref[t] == 0)
    def _():
        o_ref[...] = jnp.zeros(o_ref.shape, BF16)


def _ffn_up(x, wg, wu, tile_expert, tile_valid, tm):
    n_tiles = tile_expert.shape[0]
    k = x.shape[1]
    grid_spec = pltpu.PrefetchScalarGridSpec(
        num_scalar_prefetch=2,
        grid=(D_FF // TF, n_tiles),
        in_specs=[pl.BlockSpec((tm, k), lambda f, t, te, tv: (t, 0)),
                  pl.BlockSpec((None, k, TF), lambda f, t, te, tv: (te[t], 0, f)),
                  pl.BlockSpec((None, k, TF), lambda f, t, te, tv: (te[t], 0, f))],
        out_specs=pl.BlockSpec((tm, TF), lambda f, t, te, tv: (t, f)),
        scratch_shapes=[pltpu.VMEM((k, TF), BF16), pltpu.VMEM((k, TF), BF16)])
    return pl.pallas_call(
        _ffn_up_body,
        out_shape=jax.ShapeDtypeStruct((n_tiles * tm, D_FF), BF16),
        grid_spec=grid_spec,
        compiler_params=_params(2),
        name="ffn_up",
    )(tile_expert, tile_valid, x, wg, wu)


def _ffn_down_body(te_ref, tv_ref, h_ref, wd_ref, o_ref, wdb_ref):
    t = pl.program_id(1)

    @pl.when(_weights_changed(te_ref, t))
    def _():
        wdb_ref[...] = wd_ref[...].astype(BF16)

    @pl.when(tv_ref[t] != 0)
    def _():
        o_ref[...] = jnp.dot(h_ref[...], wdb_ref[...], preferred_element_type=F32)

    @pl.when(tv_ref[t] == 0)
    def _():
        o_ref[...] = jnp.zeros(o_ref.shape, F32)


def _ffn_down(hmid, wd, tile_expert, tile_valid, tm):
    n_tiles = tile_expert.shape[0]
    n = wd.shape[2]
    grid_spec = pltpu.PrefetchScalarGridSpec(
        num_scalar_prefetch=2,
        grid=(n // TN_DOWN, n_tiles),
        in_specs=[pl.BlockSpec((tm, D_FF), lambda j, t, te, tv: (t, 0)),
                  pl.BlockSpec((None, D_FF, TN_DOWN), lambda j, t, te, tv: (te[t], 0, j))],
        out_specs=pl.BlockSpec((tm, TN_DOWN), lambda j, t, te, tv: (t, j)),
        scratch_shapes=[pltpu.VMEM((D_FF, TN_DOWN), BF16)])
    return pl.pallas_call(
        _ffn_down_body,
        out_shape=jax.ShapeDtypeStruct((n_tiles * tm, n), F32),
        grid_spec=grid_spec,
        compiler_params=_params(2),
        name="ffn_down",
    )(tile_expert, tile_valid, hmid, wd)


def _router_body(x_ref, r_ref, idx_ref, gate_ref):
    logits = jnp.dot(x_ref[...], r_ref[...], preferred_element_type=F32, precision=lax.Precision.HIGHEST)
    lane = lax.broadcasted_iota(jnp.int32, logits.shape, 1).astype(F32)
    n = float(N_EXPERTS)
    m1 = jnp.max(logits, axis=-1, keepdims=True)
    i1 = jnp.min(jnp.where(logits == m1, lane, n), axis=-1, keepdims=True)
    rest = jnp.where(lane == i1, -jnp.inf, logits)
    m2 = jnp.max(rest, axis=-1, keepdims=True)
    i2 = jnp.min(jnp.where(rest == m2, lane, n), axis=-1, keepdims=True)
    e2 = jnp.exp(m2 - m1)
    g1 = 1.0 / (1.0 + e2)
    first = lax.broadcasted_iota(jnp.int32, idx_ref.shape, 1) == 0
    idx_ref[...] = jnp.where(first, i1, i2).astype(jnp.int32)
    gate_ref[...] = jnp.where(first, g1, e2 * g1)


def _router(x, r):
    m, k = x.shape
    out = pl.BlockSpec((TM, TOP_K), lambda i: (i, 0))
    return pl.pallas_call(
        _router_body,
        out_shape=(jax.ShapeDtypeStruct((m, TOP_K), jnp.int32), jax.ShapeDtypeStruct((m, TOP_K), F32)),
        grid=(m // TM,),
        in_specs=[pl.BlockSpec((TM, k), lambda i: (i, 0)),
                  pl.BlockSpec((k, N_EXPERTS), lambda i: (0, 0))],
        out_specs=(out, out),
        compiler_params=_params(1),
        name="router",
    )(x, r)


def _route_tables(idx, expert_base):
    a = idx.reshape(-1)
    onehot = (a[:, None] == jnp.arange(N_EXPERTS, dtype=jnp.int32)[None, :]).astype(jnp.int32)
    csum = jnp.cumsum(onehot, axis=0)
    rank = jnp.sum(onehot * csum, axis=1) - 1
    counts = csum[-1]
    padded = (counts + (TM_E - 1)) // TM_E * TM_E
    ends = jnp.cumsum(padded)
    row = jnp.sum(onehot * (ends - padded)[None, :], axis=1) + rank
    row_token = jnp.zeros((N_ROWS_E,), jnp.int32).at[row].set(jnp.arange(N_ASSIGN, dtype=jnp.int32) // TOP_K)
    tile_start = jnp.arange(N_TILES_E, dtype=jnp.int32) * TM_E
    tile_expert = jnp.sum((tile_start[:, None] >= ends[None, :]).astype(jnp.int32), axis=1)
    tile_expert = jnp.minimum(tile_expert, N_EXPERTS - 1) + expert_base
    tile_valid = (tile_start < ends[-1]).astype(jnp.int32)
    return row, row_token, tile_expert, tile_valid


def _start_row_copies(src_hbm, rows_ref, first, stride, dst, sem, n):
    def body(i, carry):
        for queue in range(N_DMA_QUEUES):
            r = N_DMA_QUEUES * i + queue
            src_row = rows_ref[first + stride * r]
            pltpu.make_async_copy(src_hbm.at[pl.ds(src_row, 1)], dst.at[pl.ds(r, 1)], sem).start(priority=queue)
        return carry

    lax.fori_loop(0, n // N_DMA_QUEUES, body, 0)


def _wait_row_copies(src_hbm, dst, sem, n):
    pltpu.make_async_copy(src_hbm.at[pl.ds(0, n)], dst, sem).wait()


def _gather_rows_body(tok_ref, x_hbm, o_ref, buf, sem):
    t = pl.program_id(0)
    n_tiles = pl.num_programs(0)

    @pl.when(t == 0)
    def _():
        _start_row_copies(x_hbm, tok_ref, 0, 1, buf.at[0], sem.at[0], TM_E)

    @pl.when(t + 1 < n_tiles)
    def _():
        nxt = (t + 1) % 2
        _start_row_copies(x_hbm, tok_ref, (t + 1) * TM_E, 1, buf.at[nxt], sem.at[nxt], TM_E)

    slot = t % 2
    _wait_row_copies(x_hbm, buf.at[slot], sem.at[slot], TM_E)
    o_ref[...] = buf[slot].astype(BF16)


def _gather_rows(x, row_token):
    d = x.shape[1]
    grid_spec = pltpu.PrefetchScalarGridSpec(
        num_scalar_prefetch=1,
        grid=(N_TILES_E,),
        in_specs=[pl.BlockSpec(memory_space=pl.ANY)],
        out_specs=pl.BlockSpec((TM_E, d), lambda t, tok: (t, 0)),
        scratch_shapes=[pltpu.VMEM((2, TM_E, d), F32), pltpu.SemaphoreType.DMA((2,))])
    return pl.pallas_call(
        _gather_rows_body,
        out_shape=jax.ShapeDtypeStruct((N_ROWS_E, d), BF16),
        grid_spec=grid_spec,
        compiler_params=_params(1),
        name="gather_rows",
    )(row_token, x)


def _combine_body(row_ref, ys_hbm, gate_ref, res_ref, g_ref, b_ref, o_ref, ob_ref, buf, sem):
    t = pl.program_id(0)
    n_tiles = pl.num_programs(0)
    tm = TM_COMB

    def start(tile, slot):
        for k in range(TOP_K):
            _start_row_copies(ys_hbm, row_ref, TOP_K * tile * tm + k, TOP_K, buf.at[slot, k], sem.at[slot], tm)

    @pl.when(t == 0)
    def _():
        start(0, 0)

    @pl.when(t + 1 < n_tiles)
    def _():
        start(t + 1, (t + 1) % 2)

    slot = t % 2
    for k in range(TOP_K):
        _wait_row_copies(ys_hbm, buf.at[slot, k], sem.at[slot], tm)
    y = gate_ref[:, 0:1] * buf[slot, 0] + gate_ref[:, 1:2] * buf[slot, 1]
    _residual_ln_store(y, res_ref, g_ref, b_ref, o_ref, ob_ref)


def _combine(ys, row, gate, res, g, b):
    m, n = res.shape
    tm = TM_COMB
    rows = lambda i, r: (i, 0)
    fixed = lambda i, r: (0, 0)
    grid_spec = pltpu.PrefetchScalarGridSpec(
        num_scalar_prefetch=1,
        grid=(m // tm,),
        in_specs=[pl.BlockSpec(memory_space=pl.ANY),
                  pl.BlockSpec((tm, TOP_K), rows),
                  pl.BlockSpec((tm, n), rows),
                  pl.BlockSpec((1, n), fixed),
                  pl.BlockSpec((1, n), fixed)],
        out_specs=(pl.BlockSpec((tm, n), rows), pl.BlockSpec((tm, n), rows)),
        scratch_shapes=[pltpu.VMEM((2, TOP_K, tm, n), F32), pltpu.SemaphoreType.DMA((2,))])
    return pl.pallas_call(
        _combine_body,
        out_shape=(jax.ShapeDtypeStruct((m, n), F32), jax.ShapeDtypeStruct((m, n), BF16)),
        grid_spec=grid_spec,
        compiler_params=_params(1),
        name="combine",
    )(row, ys, gate, res, g, b)


def kernel(x_prompt, x_sample, state_a, state_b, state_d, cache_mem_k, cache_mem_v, mem_prompt, w_in, conv_a, pool_w, pool_scale, sg_ln_g, sg_ln_b, sg_w, sg_b, conv_d, cd_ln_g, cd_ln_b, w_out, w_q, w_k, w_v, w_o, ln_g, ln_b, dense_w_gate, dense_w_up, dense_w_down, moe_router, moe_w_gate, moe_w_up, moe_w_down):
    d, g = D_MODEL, D_GRP
    x = jnp.concatenate([x_prompt.reshape(N_PROMPT, d), x_sample.reshape(DEC_BATCH, d)], axis=0)
    xb = x.astype(BF16)
    mem_b = mem_prompt.reshape(BATCH * N_MEM, d).astype(BF16)
    n_tiles = N_TOK // TM
    all_valid = jnp.ones((n_tiles,), jnp.int32)
    moe_wg = moe_w_gate.reshape(-1, d, D_FF)
    moe_wu = moe_w_up.reshape(-1, d, D_FF)
    moe_wd = moe_w_down.reshape(-1, D_FF, d)
    vec = lambda v: v.reshape(1, -1)

    sa_p, sb_p, sd_p, mk_p, mv_p, sa_s, sb_s, sd_s, sc_s = [], [], [], [], [], [], [], [], []
    for l in range(DEPTH):
        h = _matmul(xb, w_in, l, F32, TM, TN, "w_in")
        y_p, ha, hb, hd = _mix_prompt(jnp.zeros((N_TOK, d), BF16), h, conv_a[l], pool_w[l], vec(pool_scale[l]), vec(sg_ln_g[l]), vec(sg_ln_b[l]),
                                      sg_w[l], sg_b[l].T, conv_d[l], vec(cd_ln_g[l]), vec(cd_ln_b[l]))
        y_mix, ta, tb, td, tc = _mix_sample(
            y_p, h, state_a[l].reshape(DEC_BATCH, -1), state_b[l].reshape(DEC_BATCH, -1), state_d[l].reshape(DEC_BATCH, -1),
            conv_a[l], pool_w[l], vec(pool_scale[l]), vec(sg_ln_g[l]), vec(sg_ln_b[l]),
            vec(jnp.repeat(sg_w[l, :, 0, 0], CHUNK)), vec(jnp.repeat(sg_b[l, :, 0], CHUNK)),
            conv_d[l], vec(cd_ln_g[l]), vec(cd_ln_b[l]))
        x, xb = _mm_ln_resident(y_mix, w_out, l, x, vec(ln_g[l, 0]), vec(ln_b[l, 0]), "w_out_ln")

        q = _matmul(xb, w_q, l, BF16, TM, TN, "w_q")
        mk = _matmul(mem_b, w_k, l, F32, 512, TN, "w_k")
        mv = _matmul(mem_b, w_v, l, F32, 512, TN, "w_v")
        o_p = _attn_prompt(jnp.zeros((N_TOK, d), BF16), q, mk, mv)
        o_s = _attn_sample(q[N_PROMPT:].astype(F32), cache_mem_k, cache_mem_v, l)
        o = _paste_rows(o_p, o_s.astype(BF16))
        j = l // 2
        is_moe = l % 2 == 1
        x, xb = _mm_ln_resident(o, w_o, l, x, vec(ln_g[l, 1]), vec(ln_b[l, 1]), "w_o_ln")

        if not is_moe:
            dense_tiles = (jnp.full((n_tiles,), j, jnp.int32), all_valid)
            hmid = _ffn_up(xb, dense_w_gate, dense_w_up, *dense_tiles, TM)
            y = _ffn_down(hmid, dense_w_down, *dense_tiles, TM)
            x, xb = _add_ln(y, x, vec(ln_g[l, 2]), vec(ln_b[l, 2]))
        else:
            idx, gate = _router(x, moe_router[j])
            row, row_token, tile_expert, tile_valid = _route_tables(idx, j * N_EXPERTS)
            xs = _gather_rows(x, row_token)
            hmid = _ffn_up(xs, moe_wg, moe_wu, tile_expert, tile_valid, TM_E)
            ys = _ffn_down(hmid, moe_wd, tile_expert, tile_valid, TM_E)
            x, xb = _combine(ys, row, gate, x, vec(ln_g[l, 2]), vec(ln_b[l, 2]))

        sa_p.append(ha); sb_p.append(hb); sd_p.append(hd)
        mk_p.append(mk.reshape(BATCH, N_MEM, N_HEADS_X, D_HEAD_X)); mv_p.append(mv.reshape(BATCH, N_MEM, N_HEADS_X, D_HEAD_X))
        sa_s.append(ta.reshape(DEC_BATCH, CONV_A - 1, g)); sb_s.append(tb.reshape(DEC_BATCH, POOL_HIST, g))
        sd_s.append(td.reshape(DEC_BATCH, CONV_D - 1, g)); sc_s.append(tc.reshape(DEC_BATCH, 1, g))

    return (x[:N_PROMPT].reshape(BATCH, SEQ, d), x[N_PROMPT:].reshape(DEC_BATCH, 1, d),
            jnp.stack(sa_p), jnp.stack(sb_p), jnp.stack(sd_p), jnp.stack(mk_p), jnp.stack(mv_p),
            jnp.stack(sa_s), jnp.stack(sb_s), jnp.stack(sd_s), jnp.stack(sc_s))
```

```python
import functools

import jax
import jax.numpy as jnp
from jax import lax
from jax.experimental import pallas as pl
from jax.experimental.pallas import tpu as pltpu

F32 = jnp.float32
BF16 = jnp.bfloat16

D_MODEL = 2048
BATCH = 4
SEQ = 2048
DEPTH = 4
DEC_BATCH = 128
PAST_LEN = 16384
D_GRP = 512
D_IN = 8 * D_GRP
CONV_A = 3
POOL_WINDOWS = (2, 4, 8, 16)
D_POOL = 128
POOL_HIST = 15
CHUNK = 128
N_HEADS_C = 4
CONV_D = 31
N_MEM = 256
N_HEADS_X = 4
D_HEAD_X = 512
D_FF = 5632
N_EXPERTS = 8
ALPHA = (2 * DEPTH) ** 0.25
LN_EPS = 1e-5
ATTN_SCALE = D_HEAD_X ** -0.5

N_PROMPT = BATCH * SEQ
N_TOK = N_PROMPT + DEC_BATCH

VMEM_LIMIT_BYTES = 52 * 1024 * 1024

TM = 640
TM_LN = 320
TN = 1024
TF = 512
LN_ROW_GROUPS = 2
LANES = 128
TN_DOWN = 512
T_MIX = 256
SUBLANES = 8
ROWS_D = 64
LANES_D = 256
ROWS_LN = 32
HALO_A, HALO_B, HALO_D = 8, 16, 32
S_MIX = 32
TQ = 512
BS_ATT = 4
TM_E = 512
TM_COMB = 320
TOP_K = 2
N_DMA_QUEUES = 2
N_ASSIGN = TOP_K * N_TOK
N_TILES_E = -(-(N_ASSIGN + N_EXPERTS * (TM_E - 1)) // TM_E)
N_ROWS_E = N_TILES_E * TM_E


def _params(n_axes):
    return pltpu.CompilerParams(dimension_semantics=("arbitrary",) * n_axes,
                                vmem_limit_bytes=VMEM_LIMIT_BYTES)


def _layer_norm(x, g, b):
    mu = jnp.mean(x, axis=-1, keepdims=True)
    xc = x - mu
    var = jnp.mean(xc * xc, axis=-1, keepdims=True)
    return xc * lax.rsqrt(var + LN_EPS) * g + b


def _sigmoid(x):
    return 1.0 / (1.0 + jnp.exp(-x))


def _mm_body(x_ref, w_ref, o_ref, wb_ref):
    @pl.when(pl.program_id(1) == 0)
    def _():
        wb_ref[...] = w_ref[...].astype(BF16)

    o_ref[...] = jnp.dot(x_ref[...], wb_ref[...], preferred_element_type=F32).astype(o_ref.dtype)


def _matmul(x, w, layer, out_dtype, tm, tn, name):
    m, k = x.shape
    n = w.shape[2]
    return pl.pallas_call(
        _mm_body,
        out_shape=jax.ShapeDtypeStruct((m, n), out_dtype),
        grid=(n // tn, m // tm),
        in_specs=[pl.BlockSpec((tm, k), lambda j, i: (i, 0)),
                  pl.BlockSpec((None, k, tn), lambda j, i: (layer, 0, j))],
        out_specs=pl.BlockSpec((tm, tn), lambda j, i: (i, j)),
        scratch_shapes=[pltpu.VMEM((k, tn), BF16)],
        compiler_params=_params(2),
        name=name,
    )(x, w)


def _residual_ln_store(acc, res_ref, g_ref, b_ref, o_ref, ob_ref):
    y = _layer_norm(ALPHA * res_ref[...] + acc, g_ref[...], b_ref[...])
    o_ref[...] = y
    ob_ref[...] = y.astype(BF16)


def _mm_ln_resident_body(a_ref, w_ref, res_ref, g_ref, b_ref, o_ref, ob_ref, wb_ref):
    @pl.when(pl.program_id(0) == 0)
    def _():
        wb_ref[...] = w_ref[...].astype(BF16)

    rows_per_group = a_ref.shape[0] // LN_ROW_GROUPS
    for s in range(LN_ROW_GROUPS):
        rows = slice(s * rows_per_group, (s + 1) * rows_per_group)
        acc = jnp.dot(a_ref[rows, :], wb_ref[...], preferred_element_type=F32)
        y = _layer_norm(ALPHA * res_ref[rows, :] + acc, g_ref[...], b_ref[...])
        o_ref[rows, :] = y
        ob_ref[rows, :] = y.astype(BF16)


def _mm_ln_resident(a, w, layer, res, g, b, name):
    m, k = a.shape
    n = w.shape[2]
    tm = TM_LN
    row = lambda i: (i, 0)
    fixed = lambda i: (0, 0)
    return pl.pallas_call(
        _mm_ln_resident_body,
        out_shape=(jax.ShapeDtypeStruct((m, n), F32), jax.ShapeDtypeStruct((m, n), BF16)),
        grid=(m // tm,),
        in_specs=[pl.BlockSpec((tm, k), row),
                  pl.BlockSpec((None, k, n), lambda i: (layer, 0, 0), pipeline_mode=pl.Buffered(1)),
                  pl.BlockSpec((tm, n), row),
                  pl.BlockSpec((1, n), fixed),
                  pl.BlockSpec((1, n), fixed)],
        out_specs=(pl.BlockSpec((tm, n), row), pl.BlockSpec((tm, n), row)),
        scratch_shapes=[pltpu.VMEM((k, n), BF16)],
        compiler_params=_params(1),
        name=name,
    )(a, w, res, g, b)


def _add_ln_body(y_ref, res_ref, g_ref, b_ref, o_ref, ob_ref):
    _residual_ln_store(y_ref[...], res_ref, g_ref, b_ref, o_ref, ob_ref)


def _add_ln(y, res, g, b):
    m, n = res.shape
    tm = TM_LN
    row = lambda i: (i, 0)
    fixed = lambda i: (0, 0)
    return pl.pallas_call(
        _add_ln_body,
        out_shape=(jax.ShapeDtypeStruct((m, n), F32), jax.ShapeDtypeStruct((m, n), BF16)),
        grid=(m // tm,),
        in_specs=[pl.BlockSpec((tm, n), row), pl.BlockSpec((tm, n), row),
                  pl.BlockSpec((1, n), fixed), pl.BlockSpec((1, n), fixed)],
        out_specs=(pl.BlockSpec((tm, n), row), pl.BlockSpec((tm, n), row)),
        compiler_params=_params(1),
        name="add_ln",
    )(y, res, g, b)


def _mix_prompt_body(y_all_ref, h_ref, ca_ref, pw_ref, ps_ref, lg_ref, lb_ref, sw_ref, sbt_ref, cd_ref, dg_ref, db_ref,
                     y_ref, sa_ref, sb_ref, sd_ref, ea_ref, eb_ref, ed_ref, vn_ref, cv_ref, *, n_steps):
    del y_all_ref
    t = T_MIX
    g = D_GRP
    s = pl.program_id(1)

    @pl.when(s == 0)
    def _():
        ea_ref[0:HALO_A, :] = jnp.zeros((HALO_A, g), F32)
        eb_ref[0:HALO_B, :] = jnp.zeros((HALO_B, g), F32)
        ed_ref[0:HALO_D, :] = jnp.zeros((HALO_D, g), F32)

    e = h_ref[:, g:2 * g] * h_ref[:, 2 * g:3 * g]
    ea_ref[HALO_A:HALO_A + t, :] = e
    conv_a = (ca_ref[2:3, :] * e + ca_ref[1:2, :] * ea_ref[HALO_A - 1:HALO_A - 1 + t, :]
              + ca_ref[0:1, :] * ea_ref[HALO_A - 2:HALO_A - 2 + t, :])
    y_ref[:, 0:g] = (h_ref[:, 0:g] * conv_a).astype(BF16)

    eb_ref[HALO_B:HALO_B + t, :] = h_ref[:, 3 * g:4 * g]
    pos = s * t + lax.broadcasted_iota(jnp.int32, (t, 1), 0)
    for gi, w in enumerate(POOL_WINDOWS):
        lo = gi * D_POOL
        cur = h_ref[:, 3 * g + lo:3 * g + lo + D_POOL]
        win = cur
        for k in range(1, w):
            win = win + eb_ref[HALO_B - k:HALO_B - k + t, lo:lo + D_POOL]
        cnt = jnp.minimum(pos + 1, w).astype(F32)
        pooled = win / cnt - cur
        yb = jnp.dot(pooled.astype(BF16), pw_ref[gi].astype(BF16), preferred_element_type=F32)
        y_ref[:, g + lo:g + lo + D_POOL] = (yb * ps_ref[:, lo:lo + D_POOL]).astype(BF16)

    vn_ref[...] = _layer_norm(h_ref[:, 5 * g:6 * g], lg_ref[...], lb_ref[...])
    n_chunks = t // CHUNK
    rows = lax.broadcasted_iota(jnp.int32, (CHUNK, CHUNK), 0)
    cols = lax.broadcasted_iota(jnp.int32, (CHUNK, CHUNK), 1)
    for hh in range(N_HEADS_C):
        lo = hh * CHUNK
        w_tril = jnp.where(rows >= cols, sw_ref[hh], 0.0).astype(BF16)
        rhs = jnp.concatenate([vn_ref[c * CHUNK:(c + 1) * CHUNK, lo:lo + CHUNK] for c in range(n_chunks)], axis=1)
        mixed = jnp.dot(w_tril, rhs.astype(BF16), preferred_element_type=F32)
        bias = sbt_ref[:, hh:hh + 1]
        for c in range(n_chunks):
            u = h_ref[c * CHUNK:(c + 1) * CHUNK, 4 * g + lo:4 * g + lo + CHUNK]
            y_ref[c * CHUNK:(c + 1) * CHUNK, 2 * g + lo:2 * g + lo + CHUNK] = (
                u * (mixed[:, c * CHUNK:(c + 1) * CHUNK] + bias)).astype(BF16)

    ed_ref[HALO_D:HALO_D + t, :] = h_ref[:, 6 * g:7 * g] * _sigmoid(h_ref[:, 7 * g:8 * g])
    for rb in range(t // ROWS_D):
        r0 = HALO_D + rb * ROWS_D
        for lb in range(g // LANES_D):
            ls = slice(lb * LANES_D, (lb + 1) * LANES_D)
            acc = None
            for b in range(SUBLANES):
                z_b = None
                for a in range(-(-CONV_D // SUBLANES)):
                    shift = SUBLANES * a + b
                    if shift < CONV_D:
                        lo = r0 - SUBLANES * (a + 1)
                        term = cd_ref[CONV_D - 1 - shift:CONV_D - shift, ls] * ed_ref[lo:lo + ROWS_D + SUBLANES, ls]
                        z_b = term if z_b is None else z_b + term
                part = z_b[SUBLANES - b:SUBLANES - b + ROWS_D, :]
                acc = part if acc is None else acc + part
            cv_ref[rb * ROWS_D:(rb + 1) * ROWS_D, ls] = acc
    for rb in range(t // ROWS_LN):
        rows = slice(rb * ROWS_LN, (rb + 1) * ROWS_LN)
        z = _layer_norm(cv_ref[rows, :], dg_ref[...], db_ref[...])
        y_ref[rows, 3 * g:4 * g] = (z * _sigmoid(z)).astype(BF16)

    @pl.when(s == n_steps - 1)
    def _():
        sa_ref[0] = ea_ref[HALO_A + t - (CONV_A - 1):HALO_A + t, :]
        sb_ref[0] = eb_ref[HALO_B + t - POOL_HIST:HALO_B + t, :]
        sd_ref[0] = ed_ref[HALO_D + t - (CONV_D - 1):HALO_D + t, :]

    ea_ref[0:HALO_A, :] = ea_ref[t:t + HALO_A, :]
    eb_ref[0:HALO_B, :] = eb_ref[t:t + HALO_B, :]
    ed_ref[0:HALO_D, :] = ed_ref[t:t + HALO_D, :]


def _mix_prompt(y_all, h, ca, pw, ps, lg, lb, sw, sbt, cd, dg, db):
    n_steps = SEQ // T_MIX
    g = D_GRP
    c2 = lambda b, s: (0, 0)
    c3 = lambda b, s: (0, 0, 0)
    st = lambda b, s: (b, 0, 0)
    return pl.pallas_call(
        functools.partial(_mix_prompt_body, n_steps=n_steps),
        out_shape=(jax.ShapeDtypeStruct((N_TOK, D_MODEL), BF16),
                   jax.ShapeDtypeStruct((BATCH, CONV_A - 1, g), F32),
                   jax.ShapeDtypeStruct((BATCH, POOL_HIST, g), F32),
                   jax.ShapeDtypeStruct((BATCH, CONV_D - 1, g), F32)),
        grid=(BATCH, n_steps),
        input_output_aliases={0: 0},
        in_specs=[pl.BlockSpec(memory_space=pl.ANY),
                  pl.BlockSpec((T_MIX, D_IN), lambda b, s: (b * n_steps + s, 0)),
                  pl.BlockSpec((CONV_A, g), c2),
                  pl.BlockSpec((len(POOL_WINDOWS), D_POOL, D_POOL), c3),
                  pl.BlockSpec((1, g), c2),
                  pl.BlockSpec((1, g), c2),
                  pl.BlockSpec((1, g), c2),
                  pl.BlockSpec((N_HEADS_C, CHUNK, CHUNK), c3),
                  pl.BlockSpec((CHUNK, N_HEADS_C), c2),
                  pl.BlockSpec((CONV_D, g), c2),
                  pl.BlockSpec((1, g), c2),
                  pl.BlockSpec((1, g), c2)],
        out_specs=(pl.BlockSpec((T_MIX, D_MODEL), lambda b, s: (b * n_steps + s, 0)),
                   pl.BlockSpec((1, CONV_A - 1, g), st),
                   pl.BlockSpec((1, POOL_HIST, g), st),
                   pl.BlockSpec((1, CONV_D - 1, g), st)),
        scratch_shapes=[pltpu.VMEM((HALO_A + T_MIX, g), F32),
                        pltpu.VMEM((HALO_B + T_MIX, g), F32),
                        pltpu.VMEM((HALO_D + T_MIX, g), F32),
                        pltpu.VMEM((T_MIX, g), F32),
                        pltpu.VMEM((T_MIX, g), F32)],
        compiler_params=_params(2),
        name="mix_prompt",
    )(y_all, h, ca, pw, ps, lg, lb, sw, sbt, cd, dg, db)


def _mix_sample_body(y_all_ref, h_ref, sa_ref, sb_ref, sd_ref, ca_ref, pw_ref, ps_ref, lg_ref, lb_ref, w0_ref, b0_ref,
                     cd_ref, dg_ref, db_ref, y_ref, na_ref, nb_ref, nd_ref, vn_ref):
    del y_all_ref
    g = D_GRP

    e = h_ref[:, g:2 * g] * h_ref[:, 2 * g:3 * g]
    conv_a = ca_ref[2:3, :] * e + ca_ref[1:2, :] * sa_ref[:, g:2 * g] + ca_ref[0:1, :] * sa_ref[:, 0:g]
    y_ref[:, 0:g] = (h_ref[:, 0:g] * conv_a).astype(BF16)
    na_ref[:, 0:g] = sa_ref[:, g:2 * g]
    na_ref[:, g:2 * g] = e

    for gi, w in enumerate(POOL_WINDOWS):
        lo = gi * D_POOL
        cur = h_ref[:, 3 * g + lo:3 * g + lo + D_POOL]
        win = cur
        for k in range(1, w):
            col = (POOL_HIST - k) * g + lo
            win = win + sb_ref[:, col:col + D_POOL]
        cnt = float(min(PAST_LEN + 1, w))
        pooled = win / cnt - cur
        yb = jnp.dot(pooled.astype(BF16), pw_ref[gi].astype(BF16), preferred_element_type=F32)
        y_ref[:, g + lo:g + lo + D_POOL] = (yb * ps_ref[:, lo:lo + D_POOL]).astype(BF16)
    nb_ref[:, 0:(POOL_HIST - 1) * g] = sb_ref[:, g:POOL_HIST * g]
    nb_ref[:, (POOL_HIST - 1) * g:POOL_HIST * g] = h_ref[:, 3 * g:4 * g]

    vn = _layer_norm(h_ref[:, 5 * g:6 * g], lg_ref[...], lb_ref[...])
    vn_ref[...] = vn
    y_ref[:, 2 * g:3 * g] = (h_ref[:, 4 * g:5 * g] * (w0_ref[...] * vn + b0_ref[...])).astype(BF16)

    glu = h_ref[:, 6 * g:7 * g] * _sigmoid(h_ref[:, 7 * g:8 * g])
    acc = cd_ref[CONV_D - 1:CONV_D, :] * glu
    for k in range(CONV_D - 1):
        acc = acc + cd_ref[k:k + 1, :] * sd_ref[:, k * g:(k + 1) * g]
    z = _layer_norm(acc, dg_ref[...], db_ref[...])
    y_ref[:, 3 * g:4 * g] = (z * _sigmoid(z)).astype(BF16)
    nd_ref[:, 0:(CONV_D - 2) * g] = sd_ref[:, g:(CONV_D - 1) * g]
    nd_ref[:, (CONV_D - 2) * g:(CONV_D - 1) * g] = glu


def _mix_sample(y_all, h, sa, sb, sd, ca, pw, ps, lg, lb, w0, b0, cd, dg, db):
    g = D_GRP
    n_steps = DEC_BATCH // S_MIX
    first = N_PROMPT // S_MIX
    row = lambda i: (i, 0)
    c2 = lambda i: (0, 0)
    c3 = lambda i: (0, 0, 0)
    wa, wb, wd = (CONV_A - 1) * g, POOL_HIST * g, (CONV_D - 1) * g
    return pl.pallas_call(
        _mix_sample_body,
        out_shape=(jax.ShapeDtypeStruct((N_TOK, D_MODEL), BF16),
                   jax.ShapeDtypeStruct((DEC_BATCH, wa), F32),
                   jax.ShapeDtypeStruct((DEC_BATCH, wb), F32),
                   jax.ShapeDtypeStruct((DEC_BATCH, wd), F32),
                   jax.ShapeDtypeStruct((DEC_BATCH, g), F32)),
        grid=(n_steps,),
        input_output_aliases={0: 0},
        in_specs=[pl.BlockSpec(memory_space=pl.ANY),
                  pl.BlockSpec((S_MIX, D_IN), lambda i: (first + i, 0)),
                  pl.BlockSpec((S_MIX, wa), row),
                  pl.BlockSpec((S_MIX, wb), row),
                  pl.BlockSpec((S_MIX, wd), row),
                  pl.BlockSpec((CONV_A, g), c2),
                  pl.BlockSpec((len(POOL_WINDOWS), D_POOL, D_POOL), c3),
                  pl.BlockSpec((1, g), c2),
                  pl.BlockSpec((1, g), c2),
                  pl.BlockSpec((1, g), c2),
                  pl.BlockSpec((1, g), c2),
                  pl.BlockSpec((1, g), c2),
                  pl.BlockSpec((CONV_D, g), c2),
                  pl.BlockSpec((1, g), c2),
                  pl.BlockSpec((1, g), c2)],
        out_specs=(pl.BlockSpec((S_MIX, D_MODEL), lambda i: (first + i, 0)),
                   pl.BlockSpec((S_MIX, wa), row),
                   pl.BlockSpec((S_MIX, wb), row),
                   pl.BlockSpec((S_MIX, wd), row),
                   pl.BlockSpec((S_MIX, g), row)),
        compiler_params=_params(1),
        name="mix_sample",
    )(y_all, h, sa, sb, sd, ca, pw, ps, lg, lb, w0, b0, cd, dg, db)


def _attn_prompt_body(o_all_ref, q_ref, k_ref, v_ref, o_ref):
    del o_all_ref
    for hh in range(N_HEADS_X):
        sl = slice(hh * D_HEAD_X, (hh + 1) * D_HEAD_X)
        kh = k_ref[:, sl].astype(BF16)
        vh = v_ref[:, sl].astype(BF16)
        s = lax.dot_general(q_ref[:, sl], kh, (((1,), (1,)), ((), ())), preferred_element_type=F32) * ATTN_SCALE
        p = jnp.exp(s - jnp.max(s, axis=-1, keepdims=True))
        p = p / jnp.sum(p, axis=-1, keepdims=True)
        o_ref[:, sl] = jnp.dot(p.astype(BF16), vh, preferred_element_type=F32).astype(BF16)


def _attn_prompt(o_all, q, mk, mv):
    n_q = SEQ // TQ
    kv = pl.BlockSpec((N_MEM, D_MODEL), lambda b, i: (b, 0))
    qo = pl.BlockSpec((TQ, D_MODEL), lambda b, i: (b * n_q + i, 0))
    return pl.pallas_call(
        _attn_prompt_body,
        out_shape=jax.ShapeDtypeStruct((N_TOK, D_MODEL), BF16),
        grid=(BATCH, n_q),
        input_output_aliases={0: 0},
        in_specs=[pl.BlockSpec(memory_space=pl.ANY), qo, kv, kv],
        out_specs=qo,
        compiler_params=_params(2),
        name="attn_prompt",
    )(o_all, q, mk, mv)


def _paste_rows_body(all_ref, rows_ref, o_ref):
    del all_ref
    o_ref[...] = rows_ref[...]


def _paste_rows(x_all, rows):
    n, d = rows.shape
    return pl.pallas_call(
        _paste_rows_body,
        out_shape=jax.ShapeDtypeStruct(x_all.shape, x_all.dtype),
        grid=(1,),
        input_output_aliases={0: 0},
        in_specs=[pl.BlockSpec(memory_space=pl.ANY), pl.BlockSpec((n, d), lambda i: (0, 0))],
        out_specs=pl.BlockSpec((n, d), lambda i: (N_PROMPT // n, 0)),
        compiler_params=_params(1),
        name="paste_rows",
    )(x_all, rows)


N_LANE_TILES_X = D_HEAD_X // LANES
HALF = N_HEADS_X


def _attn_sample_body(q_ref, *refs):
    nc = N_LANE_TILES_X
    packed = (N_MEM // 2, 2 * N_HEADS_X, LANES)
    k_refs, v_refs, o_ref = refs[:nc], refs[nc:2 * nc], refs[2 * nc]
    for j in range(BS_ATT):
        t = k_refs[0][j].reshape(packed) * q_ref[j, 0][None]
        for c in range(1, nc):
            t = t + k_refs[c][j].reshape(packed) * q_ref[j, c][None]
        s = jnp.sum(t, axis=-1, keepdims=True) * ATTN_SCALE
        m = jnp.max(s, axis=0, keepdims=True)
        m = jnp.maximum(m, pltpu.roll(m, HALF, 1))
        p = jnp.exp(s - m)
        l = jnp.sum(p, axis=0, keepdims=True)
        p = p / (l + pltpu.roll(l, HALF, 1))
        for c in range(nc):
            acc = jnp.sum(p * v_refs[c][j].reshape(packed), axis=0)
            o_ref[j, c] = acc + pltpu.roll(acc, HALF, 0)


def _attn_sample(q, ck, cv, layer):
    nc = N_LANE_TILES_X
    q2 = q.reshape(DEC_BATCH, N_HEADS_X, nc, LANES).transpose(0, 2, 1, 3)
    q2 = jnp.concatenate([q2, q2], axis=2)
    qo = pl.BlockSpec((BS_ATT, nc, 2 * N_HEADS_X, LANES), lambda i: (i, 0, 0, 0))
    kv = [pl.BlockSpec((None, BS_ATT, N_MEM, N_HEADS_X, LANES), lambda i, c=c: (layer, i, 0, 0, c)) for c in range(nc)]
    o2 = pl.pallas_call(
        _attn_sample_body,
        out_shape=jax.ShapeDtypeStruct((DEC_BATCH, nc, 2 * N_HEADS_X, LANES), F32),
        grid=(DEC_BATCH // BS_ATT,),
        in_specs=[qo] + kv + kv,
        out_specs=qo,
        compiler_params=_params(1),
        name="attn_sample",
    )(q2, *([ck] * nc), *([cv] * nc))
    return o2[:, :, :N_HEADS_X, :].transpose(0, 2, 1, 3).reshape(DEC_BATCH, D_MODEL)


def _run_tables(tile_expert, tile_valid):
    n = tile_expert.shape[0]
    prev = jnp.concatenate([tile_expert[:1] - 1, tile_expert[:-1]])
    first = (tile_expert != prev).astype(jnp.int32)
    run = jnp.cumsum(first) - 1
    n_runs = run[-1:] + 1
    runs = jnp.arange(n, dtype=jnp.int32)
    run_expert = jnp.sum(jnp.where((run[None, :] == runs[:, None]) & (first[None, :] == 1), tile_expert[None, :], 0), axis=1)
    next_run = jnp.where(run + 1 < n_runs, run + 1, 0)
    next_expert = jnp.sum(jnp.where(next_run[:, None] == runs[None, :], run_expert[None, :], 0), axis=1)
    return tile_expert, tile_valid, first, run, next_expert.astype(jnp.int32), n_runs


def _fetch_run_weights(tabs, w_hbm, stage, wb, sem, width):
    te_ref, _, first_ref, run_ref, next_ref, nruns_ref = tabs
    col, t = pl.program_id(0), pl.program_id(1)
    n_runs = nruns_ref[0]
    seq = col * n_runs + run_ref[t]

    def copies(expert, col_block, slot):
        lanes = pl.ds(pl.multiple_of(col_block * width, width), width)
        return [pltpu.make_async_copy(w.at[expert, :, lanes], st.at[slot], sem.at[slot, i])
                for i, (w, st) in enumerate(zip(w_hbm, stage))]

    @pl.when(first_ref[t] == 1)
    def _():
        slot = seq % 2

        @pl.when(seq == 0)
        def _():
            for c in copies(te_ref[t], col, slot):
                c.start()

        for c in copies(te_ref[t], col, slot):
            c.wait()
        for st, dst in zip(stage, wb):
            dst[...] = st[slot].astype(BF16)

        last_run = run_ref[t] + 1 == n_runs

        @pl.when(jnp.logical_or(jnp.logical_not(last_run), col + 1 < pl.num_programs(0)))
        def _():
            for c in copies(next_ref[t], jnp.where(last_run, col + 1, col), 1 - slot):
                c.start()


def _ffn_up_body(te_ref, tv_ref, first_ref, run_ref, next_ref, nruns_ref, x_ref, wg_hbm, wu_hbm, o_ref,
                 sg_ref, su_ref, wgb_ref, wub_ref, sem):
    t = pl.program_id(1)
    _fetch_run_weights((te_ref, tv_ref, first_ref, run_ref, next_ref, nruns_ref), (wg_hbm, wu_hbm),
                       (sg_ref, su_ref), (wgb_ref, wub_ref), sem, TF)

    @pl.when(tv_ref[t] != 0)
    def _():
        x = x_ref[...]
        gate = jnp.dot(x, wgb_ref[...], preferred_element_type=F32)
        up = jnp.dot(x, wub_ref[...], preferred_element_type=F32)
        o_ref[...] = (gate * _sigmoid(gate) * up).astype(BF16)

    @pl.when(tv_ref[t] == 0)
    def _():
        o_ref[...] = jnp.zeros(o_ref.shape, BF16)


def _ffn_up(x, wg, wu, tabs, tm):
    n_tiles = tabs[0].shape[0]
    k = x.shape[1]
    grid_spec = pltpu.PrefetchScalarGridSpec(
        num_scalar_prefetch=len(tabs),
        grid=(D_FF // TF, n_tiles),
        in_specs=[pl.BlockSpec((tm, k), lambda f, t, *_: (t, 0)),
                  pl.BlockSpec(memory_space=pl.ANY),
                  pl.BlockSpec(memory_space=pl.ANY)],
        out_specs=pl.BlockSpec((tm, TF), lambda f, t, *_: (t, f)),
        scratch_shapes=[pltpu.VMEM((2, k, TF), F32), pltpu.VMEM((2, k, TF), F32),
                        pltpu.VMEM((k, TF), BF16), pltpu.VMEM((k, TF), BF16),
                        pltpu.SemaphoreType.DMA((2, 2))])
    return pl.pallas_call(
        _ffn_up_body,
        out_shape=jax.ShapeDtypeStruct((n_tiles * tm, D_FF), BF16),
        grid_spec=grid_spec,
        compiler_params=_params(2),
        name="ffn_up",
    )(*tabs, x, wg, wu)


def _ffn_down_body(te_ref, tv_ref, first_ref, run_ref, next_ref, nruns_ref, h_ref, wd_hbm, o_ref,
                   sd_ref, wdb_ref, sem):
    t = pl.program_id(1)
    _fetch_run_weights((te_ref, tv_ref, first_ref, run_ref, next_ref, nruns_ref), (wd_hbm,),
                       (sd_ref,), (wdb_ref,), sem, TN_DOWN)

    @pl.when(tv_ref[t] != 0)
    def _():
        o_ref[...] = jnp.dot(h_ref[...], wdb_ref[...], preferred_element_type=F32)

    @pl.when(tv_ref[t] == 0)
    def _():
        o_ref[...] = jnp.zeros(o_ref.shape, F32)


def _ffn_down(hmid, wd, tabs, tm):
    n_tiles = tabs[0].shape[0]
    n = wd.shape[2]
    grid_spec = pltpu.PrefetchScalarGridSpec(
        num_scalar_prefetch=len(tabs),
        grid=(n // TN_DOWN, n_tiles),
        in_specs=[pl.BlockSpec((tm, D_FF), lambda j, t, *_: (t, 0)),
                  pl.BlockSpec(memory_space=pl.ANY)],
        out_specs=pl.BlockSpec((tm, TN_DOWN), lambda j, t, *_: (t, j)),
        scratch_shapes=[pltpu.VMEM((2, D_FF, TN_DOWN), F32), pltpu.VMEM((D_FF, TN_DOWN), BF16),
                        pltpu.SemaphoreType.DMA((2, 1))])
    return pl.pallas_call(
        _ffn_down_body,
        out_shape=jax.ShapeDtypeStruct((n_tiles * tm, n), F32),
        grid_spec=grid_spec,
        compiler_params=_params(2),
        name="ffn_down",
    )(*tabs, hmid, wd)


def _router_body(x_ref, r_ref, idx_ref, gate_ref):
    logits = jnp.dot(x_ref[...], r_ref[...], preferred_element_type=F32, precision=lax.Precision.HIGHEST)
    lane = lax.broadcasted_iota(jnp.int32, logits.shape, 1).astype(F32)
    n = float(N_EXPERTS)
    m1 = jnp.max(logits, axis=-1, keepdims=True)
    i1 = jnp.min(jnp.where(logits == m1, lane, n), axis=-1, keepdims=True)
    rest = jnp.where(lane == i1, -jnp.inf, logits)
    m2 = jnp.max(rest, axis=-1, keepdims=True)
    i2 = jnp.min(jnp.where(rest == m2, lane, n), axis=-1, keepdims=True)
    e2 = jnp.exp(m2 - m1)
    g1 = 1.0 / (1.0 + e2)
    first = lax.broadcasted_iota(jnp.int32, idx_ref.shape, 1) == 0
    idx_ref[...] = jnp.where(first, i1, i2).astype(jnp.int32)
    gate_ref[...] = jnp.where(first, g1, e2 * g1)


def _router(x, r):
    m, k = x.shape
    out = pl.BlockSpec((TM, TOP_K), lambda i: (i, 0))
    return pl.pallas_call(
        _router_body,
        out_shape=(jax.ShapeDtypeStruct((m, TOP_K), jnp.int32), jax.ShapeDtypeStruct((m, TOP_K), F32)),
        grid=(m // TM,),
        in_specs=[pl.BlockSpec((TM, k), lambda i: (i, 0)),
                  pl.BlockSpec((k, N_EXPERTS), lambda i: (0, 0))],
        out_specs=(out, out),
        compiler_params=_params(1),
        name="router",
    )(x, r)


def _route_tables(idx, expert_base):
    a = idx.reshape(-1)
    onehot = (a[:, None] == jnp.arange(N_EXPERTS, dtype=jnp.int32)[None, :]).astype(jnp.int32)
    csum = jnp.cumsum(onehot, axis=0)
    rank = jnp.sum(onehot * csum, axis=1) - 1
    counts = csum[-1]
    padded = (counts + (TM_E - 1)) // TM_E * TM_E
    ends = jnp.cumsum(padded)
    row = jnp.sum(onehot * (ends - padded)[None, :], axis=1) + rank
    row_token = jnp.zeros((N_ROWS_E,), jnp.int32).at[row].set(jnp.arange(N_ASSIGN, dtype=jnp.int32) // TOP_K)
    tile_start = jnp.arange(N_TILES_E, dtype=jnp.int32) * TM_E
    tile_expert = jnp.sum((tile_start[:, None] >= ends[None, :]).astype(jnp.int32), axis=1)
    tile_expert = jnp.minimum(tile_expert, N_EXPERTS - 1) + expert_base
    tile_valid = (tile_start < ends[-1]).astype(jnp.int32)
    return row, row_token, tile_expert, tile_valid


def _start_row_copies(src_hbm, rows_ref, first, stride, dst, sem, n, queues):
    def body(i, carry):
        for queue in range(queues):
            r = queues * i + queue
            src_row = rows_ref[first + stride * r]
            pltpu.make_async_copy(src_hbm.at[pl.ds(src_row, 1)], dst.at[pl.ds(r, 1)], sem).start(priority=queue)
        return carry

    lax.fori_loop(0, n // queues, body, 0)


def _wait_row_copies(src_hbm, dst, sem, n):
    pltpu.make_async_copy(src_hbm.at[pl.ds(0, n)], dst, sem).wait()


def _gather_rows_body(tok_ref, tv_ref, x_hbm, o_ref, buf, sem):
    t = pl.program_id(0)
    n_tiles = pl.num_programs(0)

    @pl.when(jnp.logical_and(t == 0, tv_ref[0] != 0))
    def _():
        _start_row_copies(x_hbm, tok_ref, 0, 1, buf.at[0], sem.at[0], TM_E, 1)

    @pl.when(jnp.logical_and(t + 1 < n_tiles, tv_ref[jnp.minimum(t + 1, n_tiles - 1)] != 0))
    def _():
        nxt = (t + 1) % 2
        _start_row_copies(x_hbm, tok_ref, (t + 1) * TM_E, 1, buf.at[nxt], sem.at[nxt], TM_E, 1)

    @pl.when(tv_ref[t] != 0)
    def _():
        slot = t % 2
        _wait_row_copies(x_hbm, buf.at[slot], sem.at[slot], TM_E)
        o_ref[...] = buf[slot].astype(BF16)

    @pl.when(tv_ref[t] == 0)
    def _():
        o_ref[...] = jnp.zeros(o_ref.shape, BF16)


def _gather_rows(x, row_token, tile_valid):
    d = x.shape[1]
    grid_spec = pltpu.PrefetchScalarGridSpec(
        num_scalar_prefetch=2,
        grid=(N_TILES_E,),
        in_specs=[pl.BlockSpec(memory_space=pl.ANY)],
        out_specs=pl.BlockSpec((TM_E, d), lambda t, tok, tv: (t, 0)),
        scratch_shapes=[pltpu.VMEM((2, TM_E, d), F32), pltpu.SemaphoreType.DMA((2,))])
    return pl.pallas_call(
        _gather_rows_body,
        out_shape=jax.ShapeDtypeStruct((N_ROWS_E, d), BF16),
        grid_spec=grid_spec,
        compiler_params=_params(1),
        name="gather_rows",
    )(row_token, tile_valid, x)


def _combine_body(row_ref, ys_hbm, gate_ref, res_ref, g_ref, b_ref, o_ref, ob_ref, buf, sem):
    t = pl.program_id(0)
    n_tiles = pl.num_programs(0)
    tm = TM_COMB

    def start(tile, slot):
        for k in range(TOP_K):
            _start_row_copies(ys_hbm, row_ref, TOP_K * tile * tm + k, TOP_K, buf.at[slot, k], sem.at[slot], tm,
                              N_DMA_QUEUES)

    @pl.when(t == 0)
    def _():
        start(0, 0)

    @pl.when(t + 1 < n_tiles)
    def _():
        start(t + 1, (t + 1) % 2)

    slot = t % 2
    for k in range(TOP_K):
        _wait_row_copies(ys_hbm, buf.at[slot, k], sem.at[slot], tm)
    y = gate_ref[:, 0:1] * buf[slot, 0] + gate_ref[:, 1:2] * buf[slot, 1]
    _residual_ln_store(y, res_ref, g_ref, b_ref, o_ref, ob_ref)


def _combine(ys, row, gate, res, g, b):
    m, n = res.shape
    tm = TM_COMB
    rows = lambda i, r: (i, 0)
    fixed = lambda i, r: (0, 0)
    grid_spec = pltpu.PrefetchScalarGridSpec(
        num_scalar_prefetch=1,
        grid=(m // tm,),
        in_specs=[pl.BlockSpec(memory_space=pl.ANY),
                  pl.BlockSpec((tm, TOP_K), rows),
                  pl.BlockSpec((tm, n), rows),
                  pl.BlockSpec((1, n), fixed),
                  pl.BlockSpec((1, n), fixed)],
        out_specs=(pl.BlockSpec((tm, n), rows), pl.BlockSpec((tm, n), rows)),
        scratch_shapes=[pltpu.VMEM((2, TOP_K, tm, n), F32), pltpu.SemaphoreType.DMA((2,))])
    return pl.pallas_call(
        _combine_body,
        out_shape=(jax.ShapeDtypeStruct((m, n), F32), jax.ShapeDtypeStruct((m, n), BF16)),
        grid_spec=grid_spec,
        compiler_params=_params(1),
        name="combine",
    )(row, ys, gate, res, g, b)


def kernel(x_prompt, x_sample, state_a, state_b, state_d, cache_mem_k, cache_mem_v, mem_prompt, w_in, conv_a, pool_w, pool_scale, sg_ln_g, sg_ln_b, sg_w, sg_b, conv_d, cd_ln_g, cd_ln_b, w_out, w_q, w_k, w_v, w_o, ln_g, ln_b, dense_w_gate, dense_w_up, dense_w_down, moe_router, moe_w_gate, moe_w_up, moe_w_down):
    d, g = D_MODEL, D_GRP
    x = jnp.concatenate([x_prompt.reshape(N_PROMPT, d), x_sample.reshape(DEC_BATCH, d)], axis=0)
    xb = x.astype(BF16)
    mem_b = mem_prompt.reshape(BATCH * N_MEM, d).astype(BF16)
    n_tiles = N_TOK // TM
    all_valid = jnp.ones((n_tiles,), jnp.int32)
    moe_wg = moe_w_gate.reshape(-1, d, D_FF)
    moe_wu = moe_w_up.reshape(-1, d, D_FF)
    moe_wd = moe_w_down.reshape(-1, D_FF, d)
    vec = lambda v: v.reshape(1, -1)

    sa_p, sb_p, sd_p, mk_p, mv_p, sa_s, sb_s, sd_s, sc_s = [], [], [], [], [], [], [], [], []
    for l in range(DEPTH):
        h = _matmul(xb, w_in, l, F32, TM, TN, "w_in")
        y_p, ha, hb, hd = _mix_prompt(jnp.zeros((N_TOK, d), BF16), h, conv_a[l], pool_w[l], vec(pool_scale[l]), vec(sg_ln_g[l]), vec(sg_ln_b[l]),
                                      sg_w[l], sg_b[l].T, conv_d[l], vec(cd_ln_g[l]), vec(cd_ln_b[l]))
        y_mix, ta, tb, td, tc = _mix_sample(
            y_p, h, state_a[l].reshape(DEC_BATCH, -1), state_b[l].reshape(DEC_BATCH, -1), state_d[l].reshape(DEC_BATCH, -1),
            conv_a[l], pool_w[l], vec(pool_scale[l]), vec(sg_ln_g[l]), vec(sg_ln_b[l]),
            vec(jnp.repeat(sg_w[l, :, 0, 0], CHUNK)), vec(jnp.repeat(sg_b[l, :, 0], CHUNK)),
            conv_d[l], vec(cd_ln_g[l]), vec(cd_ln_b[l]))
        x, xb = _mm_ln_resident(y_mix, w_out, l, x, vec(ln_g[l, 0]), vec(ln_b[l, 0]), "w_out_ln")

        q = _matmul(xb, w_q, l, BF16, TM, TN, "w_q")
        mk = _matmul(mem_b, w_k, l, F32, 512, TN, "w_k")
        mv = _matmul(mem_b, w_v, l, F32, 512, TN, "w_v")
        o_p = _attn_prompt(jnp.zeros((N_TOK, d), BF16), q, mk, mv)
        o_s = _attn_sample(q[N_PROMPT:].astype(F32), cache_mem_k, cache_mem_v, l)
        o = _paste_rows(o_p, o_s.astype(BF16))
        j = l // 2
        is_moe = l % 2 == 1
        x, xb = _mm_ln_resident(o, w_o, l, x, vec(ln_g[l, 1]), vec(ln_b[l, 1]), "w_o_ln")

        if not is_moe:
            tabs = _run_tables(jnp.full((n_tiles,), j, jnp.int32), all_valid)
            hmid = _ffn_up(xb, dense_w_gate, dense_w_up, tabs, TM)
            y = _ffn_down(hmid, dense_w_down, tabs, TM)
            x, xb = _add_ln(y, x, vec(ln_g[l, 2]), vec(ln_b[l, 2]))
        else:
            idx, gate = _router(x, moe_router[j])
            row, row_token, tile_expert, tile_valid = _route_tables(idx, j * N_EXPERTS)
            tabs = _run_tables(tile_expert, tile_valid)
            xs = _gather_rows(x, row_token, tile_valid)
            hmid = _ffn_up(xs, moe_wg, moe_wu, tabs, TM_E)
            ys = _ffn_down(hmid, moe_wd, tabs, TM_E)
            x, xb = _combine(ys, row, gate, x, vec(ln_g[l, 2]), vec(ln_b[l, 2]))

        sa_p.append(ha); sb_p.append(hb); sd_p.append(hd)
        mk_p.append(mk.reshape(BATCH, N_MEM, N_HEADS_X, D_HEAD_X)); mv_p.append(mv.reshape(BATCH, N_MEM, N_HEADS_X, D_HEAD_X))
        sa_s.append(ta.reshape(DEC_BATCH, CONV_A - 1, g)); sb_s.append(tb.reshape(DEC_BATCH, POOL_HIST, g))
        sd_s.append(td.reshape(DEC_BATCH, CONV_D - 1, g)); sc_s.append(tc.reshape(DEC_BATCH, 1, g))

    return (x[:N_PROMPT].reshape(BATCH, SEQ, d), x[N_PROMPT:].reshape(DEC_BATCH, 1, d),
            jnp.stack(sa_p), jnp.stack(sb_p), jnp.stack(sd_p), jnp.stack(mk_p), jnp.stack(mv_p),
            jnp.stack(sa_s), jnp.stack(sb_s), jnp.stack(sd_s), jnp.stack(sc_s))
```

```python
import functools

import jax
import jax.numpy as jnp
from jax import lax
from jax.experimental import pallas as pl
from jax.experimental.pallas import tpu as pltpu

F32 = jnp.float32
BF16 = jnp.bfloat16

D_MODEL = 2048
BATCH = 4
SEQ = 2048
DEPTH = 4
DEC_BATCH = 128
PAST_LEN = 16384
D_GRP = 512
D_IN = 8 * D_GRP
CONV_A = 3
POOL_WINDOWS = (2, 4, 8, 16)
D_POOL = 128
POOL_HIST = 15
CHUNK = 128
N_HEADS_C = 4
CONV_D = 31
N_MEM = 256
N_HEADS_X = 4
D_HEAD_X = 512
D_FF = 5632
N_EXPERTS = 8
ALPHA = (2 * DEPTH) ** 0.25
LN_EPS = 1e-5
ATTN_SCALE = D_HEAD_X ** -0.5

N_PROMPT = BATCH * SEQ
N_TOK = N_PROMPT + DEC_BATCH

VMEM_LIMIT_BYTES = 52 * 1024 * 1024

TM = 640
TM_LN = 320
TN = 1024
TF = 512
LN_ROW_GROUPS = 2
LANES = 128
TN_DOWN = 512
T_MIX = 256
SUBLANES = 8
ROWS_D = 64
LANES_D = 256
ROWS_LN = 32
HALO_A, HALO_B, HALO_D = 8, 16, 32
S_MIX = 32
TQ = 512
BS_ATT = 4
TM_E = 512
TM_COMB = 320
TOP_K = 2
TILE_PARTS = 1
N_DMA_QUEUES = 2
N_ASSIGN = TOP_K * N_TOK
N_TILES_E = -(-(N_ASSIGN + N_EXPERTS * (TM_E - 1)) // TM_E)
N_ROWS_E = N_TILES_E * TM_E


def _params(n_axes):
    return pltpu.CompilerParams(dimension_semantics=("arbitrary",) * n_axes,
                                vmem_limit_bytes=VMEM_LIMIT_BYTES)


def _layer_spec(shape, *index):
    return pl.BlockSpec((None,) * len(index) + tuple(shape), lambda *_: tuple(index) + (0,) * len(shape))


def _layer_norm(x, g, b):
    mu = jnp.mean(x, axis=-1, keepdims=True)
    xc = x - mu
    var = jnp.mean(xc * xc, axis=-1, keepdims=True)
    return xc * lax.rsqrt(var + LN_EPS) * g + b


def _sigmoid(x):
    return 1.0 / (1.0 + jnp.exp(-x))


def _mm_body(x_ref, w_ref, o_ref, wb_ref):
    @pl.when(pl.program_id(1) == 0)
    def _():
        wb_ref[...] = w_ref[...].astype(BF16)

    o_ref[...] = jnp.dot(x_ref[...], wb_ref[...], preferred_element_type=F32).astype(o_ref.dtype)


def _matmul(x, w, layer, out_dtype, tm, tn, name):
    m, k = x.shape
    n = w.shape[2]
    return pl.pallas_call(
        _mm_body,
        out_shape=jax.ShapeDtypeStruct((m, n), out_dtype),
        grid=(n // tn, m // tm),
        in_specs=[pl.BlockSpec((tm, k), lambda j, i: (i, 0)),
                  pl.BlockSpec((None, k, tn), lambda j, i: (layer, 0, j))],
        out_specs=pl.BlockSpec((tm, tn), lambda j, i: (i, j)),
        scratch_shapes=[pltpu.VMEM((k, tn), BF16)],
        compiler_params=_params(2),
        name=name,
    )(x, w)


def _residual_ln_store(acc, res_ref, g_ref, b_ref, o_ref, ob_ref):
    y = _layer_norm(ALPHA * res_ref[...] + acc, g_ref[...], b_ref[...])
    o_ref[...] = y
    ob_ref[...] = y.astype(BF16)


def _mm_ln_resident_body(a_ref, w_ref, res_ref, g_ref, b_ref, o_ref, ob_ref, wb_ref):
    @pl.when(pl.program_id(0) == 0)
    def _():
        wb_ref[...] = w_ref[...].astype(BF16)

    rows_per_group = a_ref.shape[0] // LN_ROW_GROUPS
    for s in range(LN_ROW_GROUPS):
        rows = slice(s * rows_per_group, (s + 1) * rows_per_group)
        acc = jnp.dot(a_ref[rows, :], wb_ref[...], preferred_element_type=F32)
        y = _layer_norm(ALPHA * res_ref[rows, :] + acc, g_ref[...], b_ref[...])
        o_ref[rows, :] = y
        ob_ref[rows, :] = y.astype(BF16)


def _mm_ln_resident(a, w, layer, res, ln, name):
    m, k = a.shape
    n = w.shape[2]
    tm = TM_LN
    row = lambda i: (i, 0)
    return pl.pallas_call(
        _mm_ln_resident_body,
        out_shape=(jax.ShapeDtypeStruct((m, n), F32), jax.ShapeDtypeStruct((m, n), BF16)),
        grid=(m // tm,),
        in_specs=[pl.BlockSpec((tm, k), row),
                  pl.BlockSpec((None, k, n), lambda i: (layer, 0, 0), pipeline_mode=pl.Buffered(1)),
                  pl.BlockSpec((tm, n), row),
                  _layer_spec((1, n), ln[2]),
                  _layer_spec((1, n), ln[2])],
        out_specs=(pl.BlockSpec((tm, n), row), pl.BlockSpec((tm, n), row)),
        scratch_shapes=[pltpu.VMEM((k, n), BF16)],
        compiler_params=_params(1),
        name=name,
    )(a, w, res, ln[0], ln[1])


def _add_ln_body(y_ref, res_ref, g_ref, b_ref, o_ref, ob_ref):
    _residual_ln_store(y_ref[...], res_ref, g_ref, b_ref, o_ref, ob_ref)


def _add_ln(y, res, ln):
    m, n = res.shape
    tm = TM_LN
    row = lambda i: (i, 0)
    return pl.pallas_call(
        _add_ln_body,
        out_shape=(jax.ShapeDtypeStruct((m, n), F32), jax.ShapeDtypeStruct((m, n), BF16)),
        grid=(m // tm,),
        in_specs=[pl.BlockSpec((tm, n), row), pl.BlockSpec((tm, n), row),
                  _layer_spec((1, n), ln[2]), _layer_spec((1, n), ln[2])],
        out_specs=(pl.BlockSpec((tm, n), row), pl.BlockSpec((tm, n), row)),
        compiler_params=_params(1),
        name="add_ln",
    )(y, res, ln[0], ln[1])


def _mix_prompt_body(y_all_ref, h_ref, ca_ref, pw_ref, ps_ref, lg_ref, lb_ref, sw_ref, sbt_ref, cd_ref, dg_ref, db_ref,
                     y_ref, sa_ref, sb_ref, sd_ref, ea_ref, eb_ref, ed_ref, vn_ref, cv_ref, *, n_steps):
    del y_all_ref
    t = T_MIX
    g = D_GRP
    s = pl.program_id(1)

    @pl.when(s == 0)
    def _():
        ea_ref[0:HALO_A, :] = jnp.zeros((HALO_A, g), F32)
        eb_ref[0:HALO_B, :] = jnp.zeros((HALO_B, g), F32)
        ed_ref[0:HALO_D, :] = jnp.zeros((HALO_D, g), F32)

    e = h_ref[:, g:2 * g] * h_ref[:, 2 * g:3 * g]
    ea_ref[HALO_A:HALO_A + t, :] = e
    conv_a = (ca_ref[2:3, :] * e + ca_ref[1:2, :] * ea_ref[HALO_A - 1:HALO_A - 1 + t, :]
              + ca_ref[0:1, :] * ea_ref[HALO_A - 2:HALO_A - 2 + t, :])
    y_ref[:, 0:g] = (h_ref[:, 0:g] * conv_a).astype(BF16)

    eb_ref[HALO_B:HALO_B + t, :] = h_ref[:, 3 * g:4 * g]
    pos = s * t + lax.broadcasted_iota(jnp.int32, (t, 1), 0)
    for gi, w in enumerate(POOL_WINDOWS):
        lo = gi * D_POOL
        cur = h_ref[:, 3 * g + lo:3 * g + lo + D_POOL]
        win = cur
        for k in range(1, w):
            win = win + eb_ref[HALO_B - k:HALO_B - k + t, lo:lo + D_POOL]
        cnt = jnp.minimum(pos + 1, w).astype(F32)
        pooled = win / cnt - cur
        yb = jnp.dot(pooled.astype(BF16), pw_ref[gi].astype(BF16), preferred_element_type=F32)
        y_ref[:, g + lo:g + lo + D_POOL] = (yb * ps_ref[:, lo:lo + D_POOL]).astype(BF16)

    vn_ref[...] = _layer_norm(h_ref[:, 5 * g:6 * g], lg_ref[...], lb_ref[...])
    n_chunks = t // CHUNK
    rows = lax.broadcasted_iota(jnp.int32, (CHUNK, CHUNK), 0)
    cols = lax.broadcasted_iota(jnp.int32, (CHUNK, CHUNK), 1)
    for hh in range(N_HEADS_C):
        lo = hh * CHUNK
        w_tril = jnp.where(rows >= cols, sw_ref[hh], 0.0).astype(BF16)
        rhs = jnp.concatenate([vn_ref[c * CHUNK:(c + 1) * CHUNK, lo:lo + CHUNK] for c in range(n_chunks)], axis=1)
        mixed = jnp.dot(w_tril, rhs.astype(BF16), preferred_element_type=F32)
        bias = sbt_ref[:, hh:hh + 1]
        for c in range(n_chunks):
            u = h_ref[c * CHUNK:(c + 1) * CHUNK, 4 * g + lo:4 * g + lo + CHUNK]
            y_ref[c * CHUNK:(c + 1) * CHUNK, 2 * g + lo:2 * g + lo + CHUNK] = (
                u * (mixed[:, c * CHUNK:(c + 1) * CHUNK] + bias)).astype(BF16)

    ed_ref[HALO_D:HALO_D + t, :] = h_ref[:, 6 * g:7 * g] * _sigmoid(h_ref[:, 7 * g:8 * g])
    for rb in range(t // ROWS_D):
        r0 = HALO_D + rb * ROWS_D
        for lb in range(g // LANES_D):
            ls = slice(lb * LANES_D, (lb + 1) * LANES_D)
            acc = None
            for b in range(SUBLANES):
                z_b = None
                for a in range(-(-CONV_D // SUBLANES)):
                    shift = SUBLANES * a + b
                    if shift < CONV_D:
                        lo = r0 - SUBLANES * (a + 1)
                        term = cd_ref[CONV_D - 1 - shift:CONV_D - shift, ls] * ed_ref[lo:lo + ROWS_D + SUBLANES, ls]
                        z_b = term if z_b is None else z_b + term
                part = z_b[SUBLANES - b:SUBLANES - b + ROWS_D, :]
                acc = part if acc is None else acc + part
            cv_ref[rb * ROWS_D:(rb + 1) * ROWS_D, ls] = acc
    for rb in range(t // ROWS_LN):
        rows = slice(rb * ROWS_LN, (rb + 1) * ROWS_LN)
        z = _layer_norm(cv_ref[rows, :], dg_ref[...], db_ref[...])
        y_ref[rows, 3 * g:4 * g] = (z * _sigmoid(z)).astype(BF16)

    @pl.when(s == n_steps - 1)
    def _():
        sa_ref[0] = ea_ref[HALO_A + t - (CONV_A - 1):HALO_A + t, :]
        sb_ref[0] = eb_ref[HALO_B + t - POOL_HIST:HALO_B + t, :]
        sd_ref[0] = ed_ref[HALO_D + t - (CONV_D - 1):HALO_D + t, :]

    ea_ref[0:HALO_A, :] = ea_ref[t:t + HALO_A, :]
    eb_ref[0:HALO_B, :] = eb_ref[t:t + HALO_B, :]
    ed_ref[0:HALO_D, :] = ed_ref[t:t + HALO_D, :]


def _mix_prompt(y_all, h, layer, ca, pw, ps, lg, lb, sw, sbt, cd, dg, db):
    n_steps = SEQ // T_MIX
    g = D_GRP
    c2 = c3 = layer
    st = lambda b, s: (b, 0, 0)
    return pl.pallas_call(
        functools.partial(_mix_prompt_body, n_steps=n_steps),
        out_shape=(jax.ShapeDtypeStruct((N_TOK, D_MODEL), BF16),
                   jax.ShapeDtypeStruct((BATCH, CONV_A - 1, g), F32),
                   jax.ShapeDtypeStruct((BATCH, POOL_HIST, g), F32),
                   jax.ShapeDtypeStruct((BATCH, CONV_D - 1, g), F32)),
        grid=(BATCH, n_steps),
        input_output_aliases={0: 0},
        in_specs=[pl.BlockSpec(memory_space=pl.ANY),
                  pl.BlockSpec((T_MIX, D_IN), lambda b, s: (b * n_steps + s, 0)),
                  _layer_spec((CONV_A, g), c2),
                  _layer_spec((len(POOL_WINDOWS), D_POOL, D_POOL), c3),
                  _layer_spec((1, g), c2),
                  _layer_spec((1, g), c2),
                  _layer_spec((1, g), c2),
                  _layer_spec((N_HEADS_C, CHUNK, CHUNK), c3),
                  _layer_spec((CHUNK, N_HEADS_C), c2),
                  _layer_spec((CONV_D, g), c2),
                  _layer_spec((1, g), c2),
                  _layer_spec((1, g), c2)],
        out_specs=(pl.BlockSpec((T_MIX, D_MODEL), lambda b, s: (b * n_steps + s, 0)),
                   pl.BlockSpec((1, CONV_A - 1, g), st),
                   pl.BlockSpec((1, POOL_HIST, g), st),
                   pl.BlockSpec((1, CONV_D - 1, g), st)),
        scratch_shapes=[pltpu.VMEM((HALO_A + T_MIX, g), F32),
                        pltpu.VMEM((HALO_B + T_MIX, g), F32),
                        pltpu.VMEM((HALO_D + T_MIX, g), F32),
                        pltpu.VMEM((T_MIX, g), F32),
                        pltpu.VMEM((T_MIX, g), F32)],
        compiler_params=_params(2),
        name="mix_prompt",
    )(y_all, h, ca, pw, ps, lg, lb, sw, sbt, cd, dg, db)


def _mix_sample_body(y_all_ref, h_ref, sa_ref, sb_ref, sd_ref, ca_ref, pw_ref, ps_ref, lg_ref, lb_ref, w0_ref, b0_ref,
                     cd_ref, dg_ref, db_ref, y_ref, na_ref, nb_ref, nd_ref, vn_ref):
    del y_all_ref
    g = D_GRP

    e = h_ref[:, g:2 * g] * h_ref[:, 2 * g:3 * g]
    conv_a = ca_ref[2:3, :] * e + ca_ref[1:2, :] * sa_ref[:, g:2 * g] + ca_ref[0:1, :] * sa_ref[:, 0:g]
    y_ref[:, 0:g] = (h_ref[:, 0:g] * conv_a).astype(BF16)
    na_ref[:, 0:g] = sa_ref[:, g:2 * g]
    na_ref[:, g:2 * g] = e

    for gi, w in enumerate(POOL_WINDOWS):
        lo = gi * D_POOL
        cur = h_ref[:, 3 * g + lo:3 * g + lo + D_POOL]
        win = cur
        for k in range(1, w):
            col = (POOL_HIST - k) * g + lo
            win = win + sb_ref[:, col:col + D_POOL]
        cnt = float(min(PAST_LEN + 1, w))
        pooled = win / cnt - cur
        yb = jnp.dot(pooled.astype(BF16), pw_ref[gi].astype(BF16), preferred_element_type=F32)
        y_ref[:, g + lo:g + lo + D_POOL] = (yb * ps_ref[:, lo:lo + D_POOL]).astype(BF16)
    nb_ref[:, 0:(POOL_HIST - 1) * g] = sb_ref[:, g:POOL_HIST * g]
    nb_ref[:, (POOL_HIST - 1) * g:POOL_HIST * g] = h_ref[:, 3 * g:4 * g]

    vn = _layer_norm(h_ref[:, 5 * g:6 * g], lg_ref[...], lb_ref[...])
    vn_ref[...] = vn
    y_ref[:, 2 * g:3 * g] = (h_ref[:, 4 * g:5 * g] * (w0_ref[...] * vn + b0_ref[...])).astype(BF16)

    glu = h_ref[:, 6 * g:7 * g] * _sigmoid(h_ref[:, 7 * g:8 * g])
    acc = cd_ref[CONV_D - 1:CONV_D, :] * glu
    for k in range(CONV_D - 1):
        acc = acc + cd_ref[k:k + 1, :] * sd_ref[:, k * g:(k + 1) * g]
    z = _layer_norm(acc, dg_ref[...], db_ref[...])
    y_ref[:, 3 * g:4 * g] = (z * _sigmoid(z)).astype(BF16)
    nd_ref[:, 0:(CONV_D - 2) * g] = sd_ref[:, g:(CONV_D - 1) * g]
    nd_ref[:, (CONV_D - 2) * g:(CONV_D - 1) * g] = glu


def _mix_sample(y_all, h, sa, sb, sd, layer, ca, pw, ps, lg, lb, w0, b0, cd, dg, db):
    g = D_GRP
    n_steps = DEC_BATCH // S_MIX
    first = N_PROMPT // S_MIX
    row = lambda i: (i, 0)
    c2 = c3 = layer
    wa, wb, wd = (CONV_A - 1) * g, POOL_HIST * g, (CONV_D - 1) * g
    return pl.pallas_call(
        _mix_sample_body,
        out_shape=(jax.ShapeDtypeStruct((N_TOK, D_MODEL), BF16),
                   jax.ShapeDtypeStruct((DEC_BATCH, wa), F32),
                   jax.ShapeDtypeStruct((DEC_BATCH, wb), F32),
                   jax.ShapeDtypeStruct((DEC_BATCH, wd), F32),
                   jax.ShapeDtypeStruct((DEC_BATCH, g), F32)),
        grid=(n_steps,),
        input_output_aliases={0: 0},
        in_specs=[pl.BlockSpec(memory_space=pl.ANY),
                  pl.BlockSpec((S_MIX, D_IN), lambda i: (first + i, 0)),
                  pl.BlockSpec((S_MIX, wa), row),
                  pl.BlockSpec((S_MIX, wb), row),
                  pl.BlockSpec((S_MIX, wd), row),
                  _layer_spec((CONV_A, g), c2),
                  _layer_spec((len(POOL_WINDOWS), D_POOL, D_POOL), c3),
                  _layer_spec((1, g), c2),
                  _layer_spec((1, g), c2),
                  _layer_spec((1, g), c2),
                  _layer_spec((1, g), c2),
                  _layer_spec((1, g), c2),
                  _layer_spec((CONV_D, g), c2),
                  _layer_spec((1, g), c2),
                  _layer_spec((1, g), c2)],
        out_specs=(pl.BlockSpec((S_MIX, D_MODEL), lambda i: (first + i, 0)),
                   pl.BlockSpec((S_MIX, wa), row),
                   pl.BlockSpec((S_MIX, wb), row),
                   pl.BlockSpec((S_MIX, wd), row),
                   pl.BlockSpec((S_MIX, g), row)),
        compiler_params=_params(1),
        name="mix_sample",
    )(y_all, h, sa, sb, sd, ca, pw, ps, lg, lb, w0, b0, cd, dg, db)


def _attn_prompt_body(o_all_ref, q_ref, k_ref, v_ref, o_ref):
    del o_all_ref
    for hh in range(N_HEADS_X):
        sl = slice(hh * D_HEAD_X, (hh + 1) * D_HEAD_X)
        kh = k_ref[:, sl].astype(BF16)
        vh = v_ref[:, sl].astype(BF16)
        s = lax.dot_general(q_ref[:, sl], kh, (((1,), (1,)), ((), ())), preferred_element_type=F32) * ATTN_SCALE
        p = jnp.exp(s - jnp.max(s, axis=-1, keepdims=True))
        p = p / jnp.sum(p, axis=-1, keepdims=True)
        o_ref[:, sl] = jnp.dot(p.astype(BF16), vh, preferred_element_type=F32).astype(BF16)


def _attn_prompt(o_all, q, mk, mv):
    n_q = SEQ // TQ
    kv = pl.BlockSpec((N_MEM, D_MODEL), lambda b, i: (b, 0))
    qo = pl.BlockSpec((TQ, D_MODEL), lambda b, i: (b * n_q + i, 0))
    return pl.pallas_call(
        _attn_prompt_body,
        out_shape=jax.ShapeDtypeStruct((N_TOK, D_MODEL), BF16),
        grid=(BATCH, n_q),
        input_output_aliases={0: 0},
        in_specs=[pl.BlockSpec(memory_space=pl.ANY), qo, kv, kv],
        out_specs=qo,
        compiler_params=_params(2),
        name="attn_prompt",
    )(o_all, q, mk, mv)


def _paste_rows_body(all_ref, rows_ref, o_ref):
    del all_ref
    o_ref[...] = rows_ref[...]


def _paste_rows(x_all, rows):
    n, d = rows.shape
    return pl.pallas_call(
        _paste_rows_body,
        out_shape=jax.ShapeDtypeStruct(x_all.shape, x_all.dtype),
        grid=(1,),
        input_output_aliases={0: 0},
        in_specs=[pl.BlockSpec(memory_space=pl.ANY), pl.BlockSpec((n, d), lambda i: (0, 0))],
        out_specs=pl.BlockSpec((n, d), lambda i: (N_PROMPT // n, 0)),
        compiler_params=_params(1),
        name="paste_rows",
    )(x_all, rows)


N_LANE_TILES_X = D_HEAD_X // LANES
HALF = N_HEADS_X


def _attn_sample_body(q_ref, *refs):
    nc = N_LANE_TILES_X
    packed = (N_MEM // 2, 2 * N_HEADS_X, LANES)
    k_refs, v_refs, o_ref = refs[:nc], refs[nc:2 * nc], refs[2 * nc]
    for j in range(BS_ATT):
        t = k_refs[0][j].reshape(packed) * q_ref[j, 0][None]
        for c in range(1, nc):
            t = t + k_refs[c][j].reshape(packed) * q_ref[j, c][None]
        s = jnp.sum(t, axis=-1, keepdims=True) * ATTN_SCALE
        m = jnp.max(s, axis=0, keepdims=True)
        m = jnp.maximum(m, pltpu.roll(m, HALF, 1))
        p = jnp.exp(s - m)
        l = jnp.sum(p, axis=0, keepdims=True)
        p = p / (l + pltpu.roll(l, HALF, 1))
        for c in range(nc):
            acc = jnp.sum(p * v_refs[c][j].reshape(packed), axis=0)
            o_ref[j, c] = acc + pltpu.roll(acc, HALF, 0)


def _attn_sample(q, ck, cv, layer):
    nc = N_LANE_TILES_X
    q2 = q.reshape(DEC_BATCH, N_HEADS_X, nc, LANES).transpose(0, 2, 1, 3)
    q2 = jnp.concatenate([q2, q2], axis=2)
    qo = pl.BlockSpec((BS_ATT, nc, 2 * N_HEADS_X, LANES), lambda i: (i, 0, 0, 0))
    kv = [pl.BlockSpec((None, BS_ATT, N_MEM, N_HEADS_X, LANES), lambda i, c=c: (layer, i, 0, 0, c)) for c in range(nc)]
    o2 = pl.pallas_call(
        _attn_sample_body,
        out_shape=jax.ShapeDtypeStruct((DEC_BATCH, nc, 2 * N_HEADS_X, LANES), F32),
        grid=(DEC_BATCH // BS_ATT,),
        in_specs=[qo] + kv + kv,
        out_specs=qo,
        compiler_params=_params(1),
        name="attn_sample",
    )(q2, *([ck] * nc), *([cv] * nc))
    return o2[:, :, :N_HEADS_X, :].transpose(0, 2, 1, 3).reshape(DEC_BATCH, D_MODEL)


def _run_tables(tile_expert, tile_valid):
    n = tile_expert.shape[0]
    prev = jnp.concatenate([tile_expert[:1] - 1, tile_expert[:-1]])
    first = (tile_expert != prev).astype(jnp.int32)
    run = jnp.cumsum(first) - 1
    n_runs = run[-1:] + 1
    runs = jnp.arange(n, dtype=jnp.int32)
    run_expert = jnp.sum(jnp.where((run[None, :] == runs[:, None]) & (first[None, :] == 1), tile_expert[None, :], 0), axis=1)
    next_run = jnp.where(run + 1 < n_runs, run + 1, 0)
    next_expert = jnp.sum(jnp.where(next_run[:, None] == runs[None, :], run_expert[None, :], 0), axis=1)
    return tile_expert, tile_valid, first, run, next_expert.astype(jnp.int32), n_runs


def _fetch_run_weights(tabs, w_hbm, stage, wb, sem, width):
    te_ref, _, first_ref, run_ref, next_ref, nruns_ref = tabs
    col, t = pl.program_id(0), pl.program_id(1)
    n_runs = nruns_ref[0]
    seq = col * n_runs + run_ref[t]

    def copies(expert, col_block, slot):
        lanes = pl.ds(pl.multiple_of(col_block * width, width), width)
        return [pltpu.make_async_copy(w.at[expert, :, lanes], st.at[slot], sem.at[slot, i])
                for i, (w, st) in enumerate(zip(w_hbm, stage))]

    @pl.when(first_ref[t] == 1)
    def _():
        slot = seq % 2

        @pl.when(seq == 0)
        def _():
            for c in copies(te_ref[t], col, slot):
                c.start()

        for c in copies(te_ref[t], col, slot):
            c.wait()
        for st, dst in zip(stage, wb):
            dst[...] = st[slot].astype(BF16)

        last_run = run_ref[t] + 1 == n_runs

        @pl.when(jnp.logical_or(jnp.logical_not(last_run), col + 1 < pl.num_programs(0)))
        def _():
            for c in copies(next_ref[t], jnp.where(last_run, col + 1, col), 1 - slot):
                c.start()


def _ffn_up_body(te_ref, tv_ref, first_ref, run_ref, next_ref, nruns_ref, x_ref, wg_hbm, wu_hbm, o_ref,
                 sg_ref, su_ref, wgb_ref, wub_ref, sem):
    t = pl.program_id(1)
    _fetch_run_weights((te_ref, tv_ref, first_ref, run_ref, next_ref, nruns_ref), (wg_hbm, wu_hbm),
                       (sg_ref, su_ref), (wgb_ref, wub_ref), sem, TF)

    rows_per_part = x_ref.shape[0] // TILE_PARTS
    for part in range(TILE_PARTS):
        rows = slice(part * rows_per_part, (part + 1) * rows_per_part)

        @pl.when(tv_ref[t] > part)
        def _():
            x = x_ref[rows, :]
            gate = jnp.dot(x, wgb_ref[...], preferred_element_type=F32)
            up = jnp.dot(x, wub_ref[...], preferred_element_type=F32)
            o_ref[rows, :] = (gate * _sigmoid(gate) * up).astype(BF16)

        @pl.when(tv_ref[t] <= part)
        def _():
            o_ref[rows, :] = jnp.zeros((rows_per_part, o_ref.shape[1]), BF16)


def _ffn_up(x, wg, wu, tabs, tm):
    n_tiles = tabs[0].shape[0]
    k = x.shape[1]
    grid_spec = pltpu.PrefetchScalarGridSpec(
        num_scalar_prefetch=len(tabs),
        grid=(D_FF // TF, n_tiles),
        in_specs=[pl.BlockSpec((tm, k), lambda f, t, *_: (t, 0)),
                  pl.BlockSpec(memory_space=pl.ANY),
                  pl.BlockSpec(memory_space=pl.ANY)],
        out_specs=pl.BlockSpec((tm, TF), lambda f, t, *_: (t, f)),
        scratch_shapes=[pltpu.VMEM((2, k, TF), F32), pltpu.VMEM((2, k, TF), F32),
                        pltpu.VMEM((k, TF), BF16), pltpu.VMEM((k, TF), BF16),
                        pltpu.SemaphoreType.DMA((2, 2))])
    return pl.pallas_call(
        _ffn_up_body,
        out_shape=jax.ShapeDtypeStruct((n_tiles * tm, D_FF), BF16),
        grid_spec=grid_spec,
        compiler_params=_params(2),
        name="ffn_up",
    )(*tabs, x, wg, wu)


def _ffn_down_body(te_ref, tv_ref, first_ref, run_ref, next_ref, nruns_ref, h_ref, wd_hbm, o_ref,
                   sd_ref, wdb_ref, sem):
    t = pl.program_id(1)
    _fetch_run_weights((te_ref, tv_ref, first_ref, run_ref, next_ref, nruns_ref), (wd_hbm,),
                       (sd_ref,), (wdb_ref,), sem, TN_DOWN)

    rows_per_part = h_ref.shape[0] // TILE_PARTS
    for part in range(TILE_PARTS):
        rows = slice(part * rows_per_part, (part + 1) * rows_per_part)

        @pl.when(tv_ref[t] > part)
        def _():
            o_ref[rows, :] = jnp.dot(h_ref[rows, :], wdb_ref[...], preferred_element_type=F32)

        @pl.when(tv_ref[t] <= part)
        def _():
            o_ref[rows, :] = jnp.zeros((rows_per_part, o_ref.shape[1]), F32)


def _ffn_down(hmid, wd, tabs, tm):
    n_tiles = tabs[0].shape[0]
    n = wd.shape[2]
    grid_spec = pltpu.PrefetchScalarGridSpec(
        num_scalar_prefetch=len(tabs),
        grid=(n // TN_DOWN, n_tiles),
        in_specs=[pl.BlockSpec((tm, D_FF), lambda j, t, *_: (t, 0)),
                  pl.BlockSpec(memory_space=pl.ANY)],
        out_specs=pl.BlockSpec((tm, TN_DOWN), lambda j, t, *_: (t, j)),
        scratch_shapes=[pltpu.VMEM((2, D_FF, TN_DOWN), F32), pltpu.VMEM((D_FF, TN_DOWN), BF16),
                        pltpu.SemaphoreType.DMA((2, 1))])
    return pl.pallas_call(
        _ffn_down_body,
        out_shape=jax.ShapeDtypeStruct((n_tiles * tm, n), F32),
        grid_spec=grid_spec,
        compiler_params=_params(2),
        name="ffn_down",
    )(*tabs, hmid, wd)


def _router_body(x_ref, r_ref, idx_ref, gate_ref):
    logits = jnp.dot(x_ref[...], r_ref[...], preferred_element_type=F32, precision=lax.Precision.HIGHEST)
    lane = lax.broadcasted_iota(jnp.int32, logits.shape, 1).astype(F32)
    n = float(N_EXPERTS)
    m1 = jnp.max(logits, axis=-1, keepdims=True)
    i1 = jnp.min(jnp.where(logits == m1, lane, n), axis=-1, keepdims=True)
    rest = jnp.where(lane == i1, -jnp.inf, logits)
    m2 = jnp.max(rest, axis=-1, keepdims=True)
    i2 = jnp.min(jnp.where(rest == m2, lane, n), axis=-1, keepdims=True)
    e2 = jnp.exp(m2 - m1)
    g1 = 1.0 / (1.0 + e2)
    first = lax.broadcasted_iota(jnp.int32, idx_ref.shape, 1) == 0
    idx_ref[...] = jnp.where(first, i1, i2).astype(jnp.int32)
    gate_ref[...] = jnp.where(first, g1, e2 * g1)


def _router(x, r):
    m, k = x.shape
    out = pl.BlockSpec((TM, TOP_K), lambda i: (i, 0))
    return pl.pallas_call(
        _router_body,
        out_shape=(jax.ShapeDtypeStruct((m, TOP_K), jnp.int32), jax.ShapeDtypeStruct((m, TOP_K), F32)),
        grid=(m // TM,),
        in_specs=[pl.BlockSpec((TM, k), lambda i: (i, 0)),
                  pl.BlockSpec((k, N_EXPERTS), lambda i: (0, 0))],
        out_specs=(out, out),
        compiler_params=_params(1),
        name="router",
    )(x, r)


def _route_tables(idx, expert_base):
    a = idx.reshape(-1)
    onehot = (a[:, None] == jnp.arange(N_EXPERTS, dtype=jnp.int32)[None, :]).astype(jnp.int32)
    csum = jnp.cumsum(onehot, axis=0)
    rank = jnp.sum(onehot * csum, axis=1) - 1
    counts = csum[-1]
    padded = (counts + (TM_E - 1)) // TM_E * TM_E
    ends = jnp.cumsum(padded)
    row = jnp.sum(onehot * (ends - padded)[None, :], axis=1) + rank
    row_token = jnp.zeros((N_ROWS_E,), jnp.int32).at[row].set(jnp.arange(N_ASSIGN, dtype=jnp.int32) // TOP_K)
    tile_start = jnp.arange(N_TILES_E, dtype=jnp.int32) * TM_E
    tile_expert = jnp.sum((tile_start[:, None] >= ends[None, :]).astype(jnp.int32), axis=1)
    tile_expert = jnp.minimum(tile_expert, N_EXPERTS - 1)
    real_end = jnp.sum(jnp.where(tile_expert[:, None] == jnp.arange(N_EXPERTS)[None, :], (ends - padded + counts)[None, :], 0), axis=1)
    part = TM_E // TILE_PARTS
    tile_valid = (jnp.clip(real_end - tile_start, 0, TM_E) + (part - 1)) // part
    return row, row_token, tile_expert + expert_base, tile_valid.astype(jnp.int32)


def _start_row_copies(src_hbm, rows_ref, first, stride, dst, sem, n, queues):
    def body(i, carry):
        for queue in range(queues):
            r = queues * i + queue
            src_row = rows_ref[first + stride * r]
            pltpu.make_async_copy(src_hbm.at[pl.ds(src_row, 1)], dst.at[pl.ds(r, 1)], sem).start(priority=queue)
        return carry

    lax.fori_loop(0, n // queues, body, 0)


def _wait_row_copies(src_hbm, dst, sem, n):
    pltpu.make_async_copy(src_hbm.at[pl.ds(0, n)], dst, sem).wait()


def _gather_rows_body(tok_ref, tv_ref, x_hbm, o_ref, buf, sem):
    t = pl.program_id(0)
    n_tiles = pl.num_programs(0)

    @pl.when(jnp.logical_and(t == 0, tv_ref[0] != 0))
    def _():
        _start_row_copies(x_hbm, tok_ref, 0, 1, buf.at[0], sem.at[0], TM_E, 1)

    @pl.when(jnp.logical_and(t + 1 < n_tiles, tv_ref[jnp.minimum(t + 1, n_tiles - 1)] != 0))
    def _():
        nxt = (t + 1) % 2
        _start_row_copies(x_hbm, tok_ref, (t + 1) * TM_E, 1, buf.at[nxt], sem.at[nxt], TM_E, 1)

    @pl.when(tv_ref[t] != 0)
    def _():
        slot = t % 2
        _wait_row_copies(x_hbm, buf.at[slot], sem.at[slot], TM_E)
        o_ref[...] = buf[slot].astype(BF16)

    @pl.when(tv_ref[t] == 0)
    def _():
        o_ref[...] = jnp.zeros(o_ref.shape, BF16)


def _gather_rows(x, row_token, tile_valid):
    d = x.shape[1]
    grid_spec = pltpu.PrefetchScalarGridSpec(
        num_scalar_prefetch=2,
        grid=(N_TILES_E,),
        in_specs=[pl.BlockSpec(memory_space=pl.ANY)],
        out_specs=pl.BlockSpec((TM_E, d), lambda t, tok, tv: (t, 0)),
        scratch_shapes=[pltpu.VMEM((2, TM_E, d), F32), pltpu.SemaphoreType.DMA((2,))])
    return pl.pallas_call(
        _gather_rows_body,
        out_shape=jax.ShapeDtypeStruct((N_ROWS_E, d), BF16),
        grid_spec=grid_spec,
        compiler_params=_params(1),
        name="gather_rows",
    )(row_token, tile_valid, x)


def _combine_body(row_ref, ys_hbm, gate_ref, res_ref, g_ref, b_ref, o_ref, ob_ref, buf, sem):
    t = pl.program_id(0)
    n_tiles = pl.num_programs(0)
    tm = TM_COMB

    def start(tile, slot):
        for k in range(TOP_K):
            _start_row_copies(ys_hbm, row_ref, TOP_K * tile * tm + k, TOP_K, buf.at[slot, k], sem.at[slot], tm,
                              N_DMA_QUEUES)

    @pl.when(t == 0)
    def _():
        start(0, 0)

    @pl.when(t + 1 < n_tiles)
    def _():
        start(t + 1, (t + 1) % 2)

    slot = t % 2
    for k in range(TOP_K):
        _wait_row_copies(ys_hbm, buf.at[slot, k], sem.at[slot], tm)
    y = gate_ref[:, 0:1] * buf[slot, 0] + gate_ref[:, 1:2] * buf[slot, 1]
    _residual_ln_store(y, res_ref, g_ref, b_ref, o_ref, ob_ref)


def _combine(ys, row, gate, res, ln):
    m, n = res.shape
    tm = TM_COMB
    rows = lambda i, r: (i, 0)
    grid_spec = pltpu.PrefetchScalarGridSpec(
        num_scalar_prefetch=1,
        grid=(m // tm,),
        in_specs=[pl.BlockSpec(memory_space=pl.ANY),
                  pl.BlockSpec((tm, TOP_K), rows),
                  pl.BlockSpec((tm, n), rows),
                  _layer_spec((1, n), ln[2]),
                  _layer_spec((1, n), ln[2])],
        out_specs=(pl.BlockSpec((tm, n), rows), pl.BlockSpec((tm, n), rows)),
        scratch_shapes=[pltpu.VMEM((2, TOP_K, tm, n), F32), pltpu.SemaphoreType.DMA((2,))])
    return pl.pallas_call(
        _combine_body,
        out_shape=(jax.ShapeDtypeStruct((m, n), F32), jax.ShapeDtypeStruct((m, n), BF16)),
        grid_spec=grid_spec,
        compiler_params=_params(1),
        name="combine",
    )(row, ys, gate, res, ln[0], ln[1])


def kernel(x_prompt, x_sample, state_a, state_b, state_d, cache_mem_k, cache_mem_v, mem_prompt, w_in, conv_a, pool_w, pool_scale, sg_ln_g, sg_ln_b, sg_w, sg_b, conv_d, cd_ln_g, cd_ln_b, w_out, w_q, w_k, w_v, w_o, ln_g, ln_b, dense_w_gate, dense_w_up, dense_w_down, moe_router, moe_w_gate, moe_w_up, moe_w_down):
    d, g = D_MODEL, D_GRP
    x = jnp.concatenate([x_prompt.reshape(N_PROMPT, d), x_sample.reshape(DEC_BATCH, d)], axis=0)
    xb = x.astype(BF16)
    mem_b = mem_prompt.reshape(BATCH * N_MEM, d).astype(BF16)
    n_tiles = N_TOK // TM
    all_valid = jnp.full((n_tiles,), TILE_PARTS, jnp.int32)
    spare = jnp.zeros((N_TOK, d), BF16)
    moe_wg = moe_w_gate.reshape(-1, d, D_FF)
    moe_wu = moe_w_up.reshape(-1, d, D_FF)
    moe_wd = moe_w_down.reshape(-1, D_FF, d)
    vecs = lambda v: v.reshape(v.shape[0], 1, v.shape[1])
    ps, lg, lb, dg, db = (vecs(v) for v in (pool_scale, sg_ln_g, sg_ln_b, cd_ln_g, cd_ln_b))
    sbt = sg_b.transpose(0, 2, 1)
    w0 = vecs(jnp.repeat(sg_w[:, :, 0, 0], CHUNK, axis=1))
    b0 = vecs(jnp.repeat(sg_b[:, :, 0], CHUNK, axis=1))
    n_norms = ln_g.shape[1]
    ln_all = (ln_g.reshape(DEPTH * n_norms, 1, d), ln_b.reshape(DEPTH * n_norms, 1, d))

    sa_p, sb_p, sd_p, mk_p, mv_p, sa_s, sb_s, sd_s, sc_s = [], [], [], [], [], [], [], [], []
    for l in range(DEPTH):
        ln = lambda i: ln_all + (l * n_norms + i,)
        h = _matmul(xb, w_in, l, F32, TM, TN, "w_in")
        y_p, ha, hb, hd = _mix_prompt(spare, h, l, conv_a, pool_w, ps, lg, lb, sg_w, sbt, conv_d, dg, db)
        y_mix, ta, tb, td, tc = _mix_sample(
            y_p, h, state_a[l].reshape(DEC_BATCH, -1), state_b[l].reshape(DEC_BATCH, -1), state_d[l].reshape(DEC_BATCH, -1),
            l, conv_a, pool_w, ps, lg, lb, w0, b0, conv_d, dg, db)
        x, xb = _mm_ln_resident(y_mix, w_out, l, x, ln(0), "w_out_ln")

        q = _matmul(xb, w_q, l, BF16, TM, TN, "w_q")
        mk = _matmul(mem_b, w_k, l, F32, 512, TN, "w_k")
        mv = _matmul(mem_b, w_v, l, F32, 512, TN, "w_v")
        o_p = _attn_prompt(y_mix, q, mk, mv)
        o_s = _attn_sample(q[N_PROMPT:].astype(F32), cache_mem_k, cache_mem_v, l)
        o = _paste_rows(o_p, o_s.astype(BF16))
        j = l // 2
        is_moe = l % 2 == 1
        x, xb = _mm_ln_resident(o, w_o, l, x, ln(1), "w_o_ln")
        spare = o

        if not is_moe:
            tabs = _run_tables(jnp.full((n_tiles,), j, jnp.int32), all_valid)
            hmid = _ffn_up(xb, dense_w_gate, dense_w_up, tabs, TM)
            y = _ffn_down(hmid, dense_w_down, tabs, TM)
            x, xb = _add_ln(y, x, ln(2))
        else:
            idx, gate = _router(x, moe_router[j])
            row, row_token, tile_expert, tile_valid = _route_tables(idx, j * N_EXPERTS)
            tabs = _run_tables(tile_expert, tile_valid)
            xs = _gather_rows(x, row_token, tile_valid)
            hmid = _ffn_up(xs, moe_wg, moe_wu, tabs, TM_E)
            ys = _ffn_down(hmid, moe_wd, tabs, TM_E)
            x, xb = _combine(ys, row, gate, x, ln(2))

        sa_p.append(ha); sb_p.append(hb); sd_p.append(hd)
        mk_p.append(mk.reshape(BATCH, N_MEM, N_HEADS_X, D_HEAD_X)); mv_p.append(mv.reshape(BATCH, N_MEM, N_HEADS_X, D_HEAD_X))
        sa_s.append(ta.reshape(DEC_BATCH, CONV_A - 1, g)); sb_s.append(tb.reshape(DEC_BATCH, POOL_HIST, g))
        sd_s.append(td.reshape(DEC_BATCH, CONV_D - 1, g)); sc_s.append(tc.reshape(DEC_BATCH, 1, g))

    return (x[:N_PROMPT].reshape(BATCH, SEQ, d), x[N_PROMPT:].reshape(DEC_BATCH, 1, d),
            jnp.stack(sa_p), jnp.stack(sb_p), jnp.stack(sd_p), jnp.stack(mk_p), jnp.stack(mv_p),
            jnp.stack(sa_s), jnp.stack(sb_s), jnp.stack(sd_s), jnp.stack(sc_s))
```

```python
import functools

import jax
import jax.numpy as jnp
from jax import lax
from jax.experimental import pallas as pl
from jax.experimental.pallas import tpu as pltpu

F32 = jnp.float32
BF16 = jnp.bfloat16

D_MODEL = 2048
BATCH = 4
SEQ = 2048
DEPTH = 4
DEC_BATCH = 128
PAST_LEN = 16384
D_GRP = 512
D_IN = 8 * D_GRP
CONV_A = 3
POOL_WINDOWS = (2, 4, 8, 16)
D_POOL = 128
POOL_HIST = 15
CHUNK = 128
N_HEADS_C = 4
CONV_D = 31
N_MEM = 256
N_HEADS_X = 4
D_HEAD_X = 512
D_FF = 5632
N_EXPERTS = 8
ALPHA = (2 * DEPTH) ** 0.25
LN_EPS = 1e-5
ATTN_SCALE = D_HEAD_X ** -0.5

N_PROMPT = BATCH * SEQ
N_TOK = N_PROMPT + DEC_BATCH

VMEM_LIMIT_BYTES = 52 * 1024 * 1024

TM = 640
TM_LN = 320
TN = 1024
TF = 512
LN_ROW_GROUPS = 2
LANES = 128
TN_DOWN = 512
T_MIX = 256
SUBLANES = 8
ROWS_D = 64
LANES_D = 256
ROWS_LN = 32
HALO_A, HALO_B, HALO_D = 8, 16, 32
S_MIX = 32
TQ = 512
BS_ATT = 4
TM_E = 512
TM_COMB = 320
TOP_K = 2
TILE_PARTS = 4
N_DMA_QUEUES = 2
N_ASSIGN = TOP_K * N_TOK
N_TILES_E = -(-(N_ASSIGN + N_EXPERTS * (TM_E - 1)) // TM_E)
N_ROWS_E = N_TILES_E * TM_E


def _params(n_axes):
    return pltpu.CompilerParams(dimension_semantics=("arbitrary",) * n_axes,
                                vmem_limit_bytes=VMEM_LIMIT_BYTES)


def _layer_spec(shape, *index):
    return pl.BlockSpec((None,) * len(index) + tuple(shape), lambda *_: tuple(index) + (0,) * len(shape))


def _layer_norm(x, g, b):
    mu = jnp.mean(x, axis=-1, keepdims=True)
    xc = x - mu
    var = jnp.mean(xc * xc, axis=-1, keepdims=True)
    return xc * lax.rsqrt(var + LN_EPS) * g + b


def _sigmoid(x):
    return 1.0 / (1.0 + jnp.exp(-x))


def _mm_body(x_ref, w_ref, o_ref, wb_ref):
    @pl.when(pl.program_id(1) == 0)
    def _():
        wb_ref[...] = w_ref[...].astype(BF16)

    o_ref[...] = jnp.dot(x_ref[...], wb_ref[...], preferred_element_type=F32).astype(o_ref.dtype)


def _matmul(x, w, layer, out_dtype, tm, tn, name):
    m, k = x.shape
    n = w.shape[2]
    return pl.pallas_call(
        _mm_body,
        out_shape=jax.ShapeDtypeStruct((m, n), out_dtype),
        grid=(n // tn, m // tm),
        in_specs=[pl.BlockSpec((tm, k), lambda j, i: (i, 0)),
                  pl.BlockSpec((None, k, tn), lambda j, i: (layer, 0, j))],
        out_specs=pl.BlockSpec((tm, tn), lambda j, i: (i, j)),
        scratch_shapes=[pltpu.VMEM((k, tn), BF16)],
        compiler_params=_params(2),
        name=name,
    )(x, w)


def _residual_ln_store(acc, res_ref, g_ref, b_ref, o_ref, ob_ref):
    y = _layer_norm(ALPHA * res_ref[...] + acc, g_ref[...], b_ref[...])
    o_ref[...] = y
    ob_ref[...] = y.astype(BF16)


def _mm_ln_resident_body(a_ref, w_ref, res_ref, g_ref, b_ref, o_ref, ob_ref, wb_ref):
    @pl.when(pl.program_id(0) == 0)
    def _():
        wb_ref[...] = w_ref[...].astype(BF16)

    rows_per_group = a_ref.shape[0] // LN_ROW_GROUPS
    for s in range(LN_ROW_GROUPS):
        rows = slice(s * rows_per_group, (s + 1) * rows_per_group)
        acc = jnp.dot(a_ref[rows, :], wb_ref[...], preferred_element_type=F32)
        y = _layer_norm(ALPHA * res_ref[rows, :] + acc, g_ref[...], b_ref[...])
        o_ref[rows, :] = y
        ob_ref[rows, :] = y.astype(BF16)


def _mm_ln_resident(a, w, layer, res, ln, name):
    m, k = a.shape
    n = w.shape[2]
    tm = TM_LN
    row = lambda i: (i, 0)
    return pl.pallas_call(
        _mm_ln_resident_body,
        out_shape=(jax.ShapeDtypeStruct((m, n), F32), jax.ShapeDtypeStruct((m, n), BF16)),
        grid=(m // tm,),
        in_specs=[pl.BlockSpec((tm, k), row),
                  pl.BlockSpec((None, k, n), lambda i: (layer, 0, 0), pipeline_mode=pl.Buffered(1)),
                  pl.BlockSpec((tm, n), row),
                  _layer_spec((1, n), ln[2]),
                  _layer_spec((1, n), ln[2])],
        out_specs=(pl.BlockSpec((tm, n), row), pl.BlockSpec((tm, n), row)),
        scratch_shapes=[pltpu.VMEM((k, n), BF16)],
        compiler_params=_params(1),
        name=name,
    )(a, w, res, ln[0], ln[1])


def _add_ln_body(y_ref, res_ref, g_ref, b_ref, o_ref, ob_ref):
    _residual_ln_store(y_ref[...], res_ref, g_ref, b_ref, o_ref, ob_ref)


def _add_ln(y, res, ln):
    m, n = res.shape
    tm = TM_LN
    row = lambda i: (i, 0)
    return pl.pallas_call(
        _add_ln_body,
        out_shape=(jax.ShapeDtypeStruct((m, n), F32), jax.ShapeDtypeStruct((m, n), BF16)),
        grid=(m // tm,),
        in_specs=[pl.BlockSpec((tm, n), row), pl.BlockSpec((tm, n), row),
                  _layer_spec((1, n), ln[2]), _layer_spec((1, n), ln[2])],
        out_specs=(pl.BlockSpec((tm, n), row), pl.BlockSpec((tm, n), row)),
        compiler_params=_params(1),
        name="add_ln",
    )(y, res, ln[0], ln[1])


def _mix_prompt_body(y_all_ref, h_ref, ca_ref, pw_ref, ps_ref, lg_ref, lb_ref, sw_ref, sbt_ref, cd_ref, dg_ref, db_ref,
                     y_ref, sa_ref, sb_ref, sd_ref, ea_ref, eb_ref, ed_ref, vn_ref, cv_ref, *, n_steps):
    del y_all_ref
    t = T_MIX
    g = D_GRP
    s = pl.program_id(1)

    @pl.when(s == 0)
    def _():
        ea_ref[0:HALO_A, :] = jnp.zeros((HALO_A, g), F32)
        eb_ref[0:HALO_B, :] = jnp.zeros((HALO_B, g), F32)
        ed_ref[0:HALO_D, :] = jnp.zeros((HALO_D, g), F32)

    e = h_ref[:, g:2 * g] * h_ref[:, 2 * g:3 * g]
    ea_ref[HALO_A:HALO_A + t, :] = e
    conv_a = (ca_ref[2:3, :] * e + ca_ref[1:2, :] * ea_ref[HALO_A - 1:HALO_A - 1 + t, :]
              + ca_ref[0:1, :] * ea_ref[HALO_A - 2:HALO_A - 2 + t, :])
    y_ref[:, 0:g] = (h_ref[:, 0:g] * conv_a).astype(BF16)

    eb_ref[HALO_B:HALO_B + t, :] = h_ref[:, 3 * g:4 * g]
    pos = s * t + lax.broadcasted_iota(jnp.int32, (t, 1), 0)
    for gi, w in enumerate(POOL_WINDOWS):
        lo = gi * D_POOL
        cur = h_ref[:, 3 * g + lo:3 * g + lo + D_POOL]
        win = cur
        for k in range(1, w):
            win = win + eb_ref[HALO_B - k:HALO_B - k + t, lo:lo + D_POOL]
        cnt = jnp.minimum(pos + 1, w).astype(F32)
        pooled = win / cnt - cur
        yb = jnp.dot(pooled.astype(BF16), pw_ref[gi].astype(BF16), preferred_element_type=F32)
        y_ref[:, g + lo:g + lo + D_POOL] = (yb * ps_ref[:, lo:lo + D_POOL]).astype(BF16)

    vn_ref[...] = _layer_norm(h_ref[:, 5 * g:6 * g], lg_ref[...], lb_ref[...])
    n_chunks = t // CHUNK
    rows = lax.broadcasted_iota(jnp.int32, (CHUNK, CHUNK), 0)
    cols = lax.broadcasted_iota(jnp.int32, (CHUNK, CHUNK), 1)
    for hh in range(N_HEADS_C):
        lo = hh * CHUNK
        w_tril = jnp.where(rows >= cols, sw_ref[hh], 0.0).astype(BF16)
        rhs = jnp.concatenate([vn_ref[c * CHUNK:(c + 1) * CHUNK, lo:lo + CHUNK] for c in range(n_chunks)], axis=1)
        mixed = jnp.dot(w_tril, rhs.astype(BF16), preferred_element_type=F32)
        bias = sbt_ref[:, hh:hh + 1]
        for c in range(n_chunks):
            u = h_ref[c * CHUNK:(c + 1) * CHUNK, 4 * g + lo:4 * g + lo + CHUNK]
            y_ref[c * CHUNK:(c + 1) * CHUNK, 2 * g + lo:2 * g + lo + CHUNK] = (
                u * (mixed[:, c * CHUNK:(c + 1) * CHUNK] + bias)).astype(BF16)

    ed_ref[HALO_D:HALO_D + t, :] = h_ref[:, 6 * g:7 * g] * _sigmoid(h_ref[:, 7 * g:8 * g])
    for rb in range(t // ROWS_D):
        r0 = HALO_D + rb * ROWS_D
        for lb in range(g // LANES_D):
            ls = slice(lb * LANES_D, (lb + 1) * LANES_D)
            acc = None
            for b in range(SUBLANES):
                z_b = None
                for a in range(-(-CONV_D // SUBLANES)):
                    shift = SUBLANES * a + b
                    if shift < CONV_D:
                        lo = r0 - SUBLANES * (a + 1)
                        term = cd_ref[CONV_D - 1 - shift:CONV_D - shift, ls] * ed_ref[lo:lo + ROWS_D + SUBLANES, ls]
                        z_b = term if z_b is None else z_b + term
                part = z_b[SUBLANES - b:SUBLANES - b + ROWS_D, :]
                acc = part if acc is None else acc + part
            cv_ref[rb * ROWS_D:(rb + 1) * ROWS_D, ls] = acc
    for rb in range(t // ROWS_LN):
        rows = slice(rb * ROWS_LN, (rb + 1) * ROWS_LN)
        z = _layer_norm(cv_ref[rows, :], dg_ref[...], db_ref[...])
        y_ref[rows, 3 * g:4 * g] = (z * _sigmoid(z)).astype(BF16)

    @pl.when(s == n_steps - 1)
    def _():
        sa_ref[0] = ea_ref[HALO_A + t - (CONV_A - 1):HALO_A + t, :]
        sb_ref[0] = eb_ref[HALO_B + t - POOL_HIST:HALO_B + t, :]
        sd_ref[0] = ed_ref[HALO_D + t - (CONV_D - 1):HALO_D + t, :]

    ea_ref[0:HALO_A, :] = ea_ref[t:t + HALO_A, :]
    eb_ref[0:HALO_B, :] = eb_ref[t:t + HALO_B, :]
    ed_ref[0:HALO_D, :] = ed_ref[t:t + HALO_D, :]


def _mix_prompt(y_all, h, layer, ca, pw, ps, lg, lb, sw, sbt, cd, dg, db):
    n_steps = SEQ // T_MIX
    g = D_GRP
    c2 = c3 = layer
    st = lambda b, s: (b, 0, 0)
    return pl.pallas_call(
        functools.partial(_mix_prompt_body, n_steps=n_steps),
        out_shape=(jax.ShapeDtypeStruct((N_TOK, D_MODEL), BF16),
                   jax.ShapeDtypeStruct((BATCH, CONV_A - 1, g), F32),
                   jax.ShapeDtypeStruct((BATCH, POOL_HIST, g), F32),
                   jax.ShapeDtypeStruct((BATCH, CONV_D - 1, g), F32)),
        grid=(BATCH, n_steps),
        input_output_aliases={0: 0},
        in_specs=[pl.BlockSpec(memory_space=pl.ANY),
                  pl.BlockSpec((T_MIX, D_IN), lambda b, s: (b * n_steps + s, 0)),
                  _layer_spec((CONV_A, g), c2),
                  _layer_spec((len(POOL_WINDOWS), D_POOL, D_POOL), c3),
                  _layer_spec((1, g), c2),
                  _layer_spec((1, g), c2),
                  _layer_spec((1, g), c2),
                  _layer_spec((N_HEADS_C, CHUNK, CHUNK), c3),
                  _layer_spec((CHUNK, N_HEADS_C), c2),
                  _layer_spec((CONV_D, g), c2),
                  _layer_spec((1, g), c2),
                  _layer_spec((1, g), c2)],
        out_specs=(pl.BlockSpec((T_MIX, D_MODEL), lambda b, s: (b * n_steps + s, 0)),
                   pl.BlockSpec((1, CONV_A - 1, g), st),
                   pl.BlockSpec((1, POOL_HIST, g), st),
                   pl.BlockSpec((1, CONV_D - 1, g), st)),
        scratch_shapes=[pltpu.VMEM((HALO_A + T_MIX, g), F32),
                        pltpu.VMEM((HALO_B + T_MIX, g), F32),
                        pltpu.VMEM((HALO_D + T_MIX, g), F32),
                        pltpu.VMEM((T_MIX, g), F32),
                        pltpu.VMEM((T_MIX, g), F32)],
        compiler_params=_params(2),
        name="mix_prompt",
    )(y_all, h, ca, pw, ps, lg, lb, sw, sbt, cd, dg, db)


def _mix_sample_body(y_all_ref, h_ref, sa_ref, sb_ref, sd_ref, ca_ref, pw_ref, ps_ref, lg_ref, lb_ref, w0_ref, b0_ref,
                     cd_ref, dg_ref, db_ref, y_ref, na_ref, nb_ref, nd_ref, vn_ref):
    del y_all_ref
    g = D_GRP

    e = h_ref[:, g:2 * g] * h_ref[:, 2 * g:3 * g]
    conv_a = ca_ref[2:3, :] * e + ca_ref[1:2, :] * sa_ref[:, g:2 * g] + ca_ref[0:1, :] * sa_ref[:, 0:g]
    y_ref[:, 0:g] = (h_ref[:, 0:g] * conv_a).astype(BF16)
    na_ref[:, 0:g] = sa_ref[:, g:2 * g]
    na_ref[:, g:2 * g] = e

    for gi, w in enumerate(POOL_WINDOWS):
        lo = gi * D_POOL
        cur = h_ref[:, 3 * g + lo:3 * g + lo + D_POOL]
        win = cur
        for k in range(1, w):
            col = (POOL_HIST - k) * g + lo
            win = win + sb_ref[:, col:col + D_POOL]
        cnt = float(min(PAST_LEN + 1, w))
        pooled = win / cnt - cur
        yb = jnp.dot(pooled.astype(BF16), pw_ref[gi].astype(BF16), preferred_element_type=F32)
        y_ref[:, g + lo:g + lo + D_POOL] = (yb * ps_ref[:, lo:lo + D_POOL]).astype(BF16)
    nb_ref[:, 0:(POOL_HIST - 1) * g] = sb_ref[:, g:POOL_HIST * g]
    nb_ref[:, (POOL_HIST - 1) * g:POOL_HIST * g] = h_ref[:, 3 * g:4 * g]

    vn = _layer_norm(h_ref[:, 5 * g:6 * g], lg_ref[...], lb_ref[...])
    vn_ref[...] = vn
    y_ref[:, 2 * g:3 * g] = (h_ref[:, 4 * g:5 * g] * (w0_ref[...] * vn + b0_ref[...])).astype(BF16)

    glu = h_ref[:, 6 * g:7 * g] * _sigmoid(h_ref[:, 7 * g:8 * g])
    acc = cd_ref[CONV_D - 1:CONV_D, :] * glu
    for k in range(CONV_D - 1):
        acc = acc + cd_ref[k:k + 1, :] * sd_ref[:, k * g:(k + 1) * g]
    z = _layer_norm(acc, dg_ref[...], db_ref[...])
    y_ref[:, 3 * g:4 * g] = (z * _sigmoid(z)).astype(BF16)
    nd_ref[:, 0:(CONV_D - 2) * g] = sd_ref[:, g:(CONV_D - 1) * g]
    nd_ref[:, (CONV_D - 2) * g:(CONV_D - 1) * g] = glu


def _mix_sample(y_all, h, sa, sb, sd, layer, ca, pw, ps, lg, lb, w0, b0, cd, dg, db):
    g = D_GRP
    n_steps = DEC_BATCH // S_MIX
    first = N_PROMPT // S_MIX
    row = lambda i: (i, 0)
    c2 = c3 = layer
    wa, wb, wd = (CONV_A - 1) * g, POOL_HIST * g, (CONV_D - 1) * g
    return pl.pallas_call(
        _mix_sample_body,
        out_shape=(jax.ShapeDtypeStruct((N_TOK, D_MODEL), BF16),
                   jax.ShapeDtypeStruct((DEC_BATCH, wa), F32),
                   jax.ShapeDtypeStruct((DEC_BATCH, wb), F32),
                   jax.ShapeDtypeStruct((DEC_BATCH, wd), F32),
                   jax.ShapeDtypeStruct((DEC_BATCH, g), F32)),
        grid=(n_steps,),
        input_output_aliases={0: 0},
        in_specs=[pl.BlockSpec(memory_space=pl.ANY),
                  pl.BlockSpec((S_MIX, D_IN), lambda i: (first + i, 0)),
                  pl.BlockSpec((S_MIX, wa), row),
                  pl.BlockSpec((S_MIX, wb), row),
                  pl.BlockSpec((S_MIX, wd), row),
                  _layer_spec((CONV_A, g), c2),
                  _layer_spec((len(POOL_WINDOWS), D_POOL, D_POOL), c3),
                  _layer_spec((1, g), c2),
                  _layer_spec((1, g), c2),
                  _layer_spec((1, g), c2),
                  _layer_spec((1, g), c2),
                  _layer_spec((1, g), c2),
                  _layer_spec((CONV_D, g), c2),
                  _layer_spec((1, g), c2),
                  _layer_spec((1, g), c2)],
        out_specs=(pl.BlockSpec((S_MIX, D_MODEL), lambda i: (first + i, 0)),
                   pl.BlockSpec((S_MIX, wa), row),
                   pl.BlockSpec((S_MIX, wb), row),
                   pl.BlockSpec((S_MIX, wd), row),
                   pl.BlockSpec((S_MIX, g), row)),
        compiler_params=_params(1),
        name="mix_sample",
    )(y_all, h, sa, sb, sd, ca, pw, ps, lg, lb, w0, b0, cd, dg, db)


def _attn_prompt_body(o_all_ref, q_ref, k_ref, v_ref, o_ref):
    del o_all_ref
    for hh in range(N_HEADS_X):
        sl = slice(hh * D_HEAD_X, (hh + 1) * D_HEAD_X)
        kh = k_ref[:, sl].astype(BF16)
        vh = v_ref[:, sl].astype(BF16)
        s = lax.dot_general(q_ref[:, sl], kh, (((1,), (1,)), ((), ())), preferred_element_type=F32) * ATTN_SCALE
        p = jnp.exp(s - jnp.max(s, axis=-1, keepdims=True))
        p = p / jnp.sum(p, axis=-1, keepdims=True)
        o_ref[:, sl] = jnp.dot(p.astype(BF16), vh, preferred_element_type=F32).astype(BF16)


def _attn_prompt(o_all, q, mk, mv):
    n_q = SEQ // TQ
    kv = pl.BlockSpec((N_MEM, D_MODEL), lambda b, i: (b, 0))
    qo = pl.BlockSpec((TQ, D_MODEL), lambda b, i: (b * n_q + i, 0))
    return pl.pallas_call(
        _attn_prompt_body,
        out_shape=jax.ShapeDtypeStruct((N_TOK, D_MODEL), BF16),
        grid=(BATCH, n_q),
        input_output_aliases={0: 0},
        in_specs=[pl.BlockSpec(memory_space=pl.ANY), qo, kv, kv],
        out_specs=qo,
        compiler_params=_params(2),
        name="attn_prompt",
    )(o_all, q, mk, mv)


def _paste_rows_body(all_ref, rows_ref, o_ref):
    del all_ref
    o_ref[...] = rows_ref[...]


def _paste_rows(x_all, rows):
    n, d = rows.shape
    return pl.pallas_call(
        _paste_rows_body,
        out_shape=jax.ShapeDtypeStruct(x_all.shape, x_all.dtype),
        grid=(1,),
        input_output_aliases={0: 0},
        in_specs=[pl.BlockSpec(memory_space=pl.ANY), pl.BlockSpec((n, d), lambda i: (0, 0))],
        out_specs=pl.BlockSpec((n, d), lambda i: (N_PROMPT // n, 0)),
        compiler_params=_params(1),
        name="paste_rows",
    )(x_all, rows)


N_LANE_TILES_X = D_HEAD_X // LANES
HALF = N_HEADS_X


def _attn_sample_body(q_ref, *refs):
    nc = N_LANE_TILES_X
    packed = (N_MEM // 2, 2 * N_HEADS_X, LANES)
    k_refs, v_refs, o_ref = refs[:nc], refs[nc:2 * nc], refs[2 * nc]
    for j in range(BS_ATT):
        t = k_refs[0][j].reshape(packed) * q_ref[j, 0][None]
        for c in range(1, nc):
            t = t + k_refs[c][j].reshape(packed) * q_ref[j, c][None]
        s = jnp.sum(t, axis=-1, keepdims=True) * ATTN_SCALE
        m = jnp.max(s, axis=0, keepdims=True)
        m = jnp.maximum(m, pltpu.roll(m, HALF, 1))
        p = jnp.exp(s - m)
        l = jnp.sum(p, axis=0, keepdims=True)
        p = p / (l + pltpu.roll(l, HALF, 1))
        for c in range(nc):
            acc = jnp.sum(p * v_refs[c][j].reshape(packed), axis=0)
            o_ref[j, c] = acc + pltpu.roll(acc, HALF, 0)


def _attn_sample(q, ck, cv, layer):
    nc = N_LANE_TILES_X
    q2 = q.reshape(DEC_BATCH, N_HEADS_X, nc, LANES).transpose(0, 2, 1, 3)
    q2 = jnp.concatenate([q2, q2], axis=2)
    qo = pl.BlockSpec((BS_ATT, nc, 2 * N_HEADS_X, LANES), lambda i: (i, 0, 0, 0))
    kv = [pl.BlockSpec((None, BS_ATT, N_MEM, N_HEADS_X, LANES), lambda i, c=c: (layer, i, 0, 0, c)) for c in range(nc)]
    o2 = pl.pallas_call(
        _attn_sample_body,
        out_shape=jax.ShapeDtypeStruct((DEC_BATCH, nc, 2 * N_HEADS_X, LANES), F32),
        grid=(DEC_BATCH // BS_ATT,),
        in_specs=[qo] + kv + kv,
        out_specs=qo,
        compiler_params=_params(1),
        name="attn_sample",
    )(q2, *([ck] * nc), *([cv] * nc))
    return o2[:, :, :N_HEADS_X, :].transpose(0, 2, 1, 3).reshape(DEC_BATCH, D_MODEL)


def _run_tables(tile_expert, tile_valid):
    n = tile_expert.shape[0]
    prev = jnp.concatenate([tile_expert[:1] - 1, tile_expert[:-1]])
    first = (tile_expert != prev).astype(jnp.int32)
    run = jnp.cumsum(first) - 1
    n_runs = run[-1:] + 1
    runs = jnp.arange(n, dtype=jnp.int32)
    run_expert = jnp.sum(jnp.where((run[None, :] == runs[:, None]) & (first[None, :] == 1), tile_expert[None, :], 0), axis=1)
    next_run = jnp.where(run + 1 < n_runs, run + 1, 0)
    next_expert = jnp.sum(jnp.where(next_run[:, None] == runs[None, :], run_expert[None, :], 0), axis=1)
    return tile_expert, tile_valid, first, run, next_expert.astype(jnp.int32), n_runs


def _fetch_run_weights(tabs, w_hbm, stage, wb, sem, width):
    te_ref, _, first_ref, run_ref, next_ref, nruns_ref = tabs
    col, t = pl.program_id(0), pl.program_id(1)
    n_runs = nruns_ref[0]
    seq = col * n_runs + run_ref[t]

    def copies(expert, col_block, slot):
        lanes = pl.ds(pl.multiple_of(col_block * width, width), width)
        return [pltpu.make_async_copy(w.at[expert, :, lanes], st.at[slot], sem.at[slot, i])
                for i, (w, st) in enumerate(zip(w_hbm, stage))]

    @pl.when(first_ref[t] == 1)
    def _():
        slot = seq % 2

        @pl.when(seq == 0)
        def _():
            for c in copies(te_ref[t], col, slot):
                c.start()

        for c in copies(te_ref[t], col, slot):
            c.wait()
        for st, dst in zip(stage, wb):
            dst[...] = st[slot].astype(BF16)

        last_run = run_ref[t] + 1 == n_runs

        @pl.when(jnp.logical_or(jnp.logical_not(last_run), col + 1 < pl.num_programs(0)))
        def _():
            for c in copies(next_ref[t], jnp.where(last_run, col + 1, col), 1 - slot):
                c.start()


def _for_real_rows(n_parts, o_ref, compute):
    tile_rows = o_ref.shape[0]
    rows_per_part = tile_rows // TILE_PARTS
    for count in range(TILE_PARTS + 1):
        real = count * rows_per_part

        @pl.when(n_parts == count)
        def _():
            if real > 0:
                compute(slice(0, real))
            if real < tile_rows:
                o_ref[real:tile_rows, :] = jnp.zeros((tile_rows - real, o_ref.shape[1]), o_ref.dtype)


def _ffn_up_body(te_ref, tv_ref, first_ref, run_ref, next_ref, nruns_ref, x_ref, wg_hbm, wu_hbm, o_ref,
                 sg_ref, su_ref, wgb_ref, wub_ref, sem):
    t = pl.program_id(1)
    _fetch_run_weights((te_ref, tv_ref, first_ref, run_ref, next_ref, nruns_ref), (wg_hbm, wu_hbm),
                       (sg_ref, su_ref), (wgb_ref, wub_ref), sem, TF)

    def compute(rows):
        x = x_ref[rows, :]
        gate = jnp.dot(x, wgb_ref[...], preferred_element_type=F32)
        up = jnp.dot(x, wub_ref[...], preferred_element_type=F32)
        o_ref[rows, :] = (gate * _sigmoid(gate) * up).astype(BF16)

    _for_real_rows(tv_ref[t], o_ref, compute)


def _ffn_up(x, wg, wu, tabs, tm):
    n_tiles = tabs[0].shape[0]
    k = x.shape[1]
    grid_spec = pltpu.PrefetchScalarGridSpec(
        num_scalar_prefetch=len(tabs),
        grid=(D_FF // TF, n_tiles),
        in_specs=[pl.BlockSpec((tm, k), lambda f, t, *_: (t, 0)),
                  pl.BlockSpec(memory_space=pl.ANY),
                  pl.BlockSpec(memory_space=pl.ANY)],
        out_specs=pl.BlockSpec((tm, TF), lambda f, t, *_: (t, f)),
        scratch_shapes=[pltpu.VMEM((2, k, TF), F32), pltpu.VMEM((2, k, TF), F32),
                        pltpu.VMEM((k, TF), BF16), pltpu.VMEM((k, TF), BF16),
                        pltpu.SemaphoreType.DMA((2, 2))])
    return pl.pallas_call(
        _ffn_up_body,
        out_shape=jax.ShapeDtypeStruct((n_tiles * tm, D_FF), BF16),
        grid_spec=grid_spec,
        compiler_params=_params(2),
        name="ffn_up",
    )(*tabs, x, wg, wu)


def _ffn_down_body(te_ref, tv_ref, first_ref, run_ref, next_ref, nruns_ref, h_ref, wd_hbm, o_ref,
                   sd_ref, wdb_ref, sem):
    t = pl.program_id(1)
    _fetch_run_weights((te_ref, tv_ref, first_ref, run_ref, next_ref, nruns_ref), (wd_hbm,),
                       (sd_ref,), (wdb_ref,), sem, TN_DOWN)

    def compute(rows):
        o_ref[rows, :] = jnp.dot(h_ref[rows, :], wdb_ref[...], preferred_element_type=F32)

    _for_real_rows(tv_ref[t], o_ref, compute)


def _ffn_down(hmid, wd, tabs, tm):
    n_tiles = tabs[0].shape[0]
    n = wd.shape[2]
    grid_spec = pltpu.PrefetchScalarGridSpec(
        num_scalar_prefetch=len(tabs),
        grid=(n // TN_DOWN, n_tiles),
        in_specs=[pl.BlockSpec((tm, D_FF), lambda j, t, *_: (t, 0)),
                  pl.BlockSpec(memory_space=pl.ANY)],
        out_specs=pl.BlockSpec((tm, TN_DOWN), lambda j, t, *_: (t, j)),
        scratch_shapes=[pltpu.VMEM((2, D_FF, TN_DOWN), F32), pltpu.VMEM((D_FF, TN_DOWN), BF16),
                        pltpu.SemaphoreType.DMA((2, 1))])
    return pl.pallas_call(
        _ffn_down_body,
        out_shape=jax.ShapeDtypeStruct((n_tiles * tm, n), F32),
        grid_spec=grid_spec,
        compiler_params=_params(2),
        name="ffn_down",
    )(*tabs, hmid, wd)


def _router_body(x_ref, r_ref, idx_ref, gate_ref):
    logits = jnp.dot(x_ref[...], r_ref[...], preferred_element_type=F32, precision=lax.Precision.HIGHEST)
    lane = lax.broadcasted_iota(jnp.int32, logits.shape, 1).astype(F32)
    n = float(N_EXPERTS)
    m1 = jnp.max(logits, axis=-1, keepdims=True)
    i1 = jnp.min(jnp.where(logits == m1, lane, n), axis=-1, keepdims=True)
    rest = jnp.where(lane == i1, -jnp.inf, logits)
    m2 = jnp.max(rest, axis=-1, keepdims=True)
    i2 = jnp.min(jnp.where(rest == m2, lane, n), axis=-1, keepdims=True)
    e2 = jnp.exp(m2 - m1)
    g1 = 1.0 / (1.0 + e2)
    first = lax.broadcasted_iota(jnp.int32, idx_ref.shape, 1) == 0
    idx_ref[...] = jnp.where(first, i1, i2).astype(jnp.int32)
    gate_ref[...] = jnp.where(first, g1, e2 * g1)


def _router(x, r):
    m, k = x.shape
    out = pl.BlockSpec((TM, TOP_K), lambda i: (i, 0))
    return pl.pallas_call(
        _router_body,
        out_shape=(jax.ShapeDtypeStruct((m, TOP_K), jnp.int32), jax.ShapeDtypeStruct((m, TOP_K), F32)),
        grid=(m // TM,),
        in_specs=[pl.BlockSpec((TM, k), lambda i: (i, 0)),
                  pl.BlockSpec((k, N_EXPERTS), lambda i: (0, 0))],
        out_specs=(out, out),
        compiler_params=_params(1),
        name="router",
    )(x, r)


def _route_tables(idx, expert_base):
    a = idx.reshape(-1)
    onehot = (a[:, None] == jnp.arange(N_EXPERTS, dtype=jnp.int32)[None, :]).astype(jnp.int32)
    csum = jnp.cumsum(onehot, axis=0)
    rank = jnp.sum(onehot * csum, axis=1) - 1
    counts = csum[-1]
    padded = (counts + (TM_E - 1)) // TM_E * TM_E
    ends = jnp.cumsum(padded)
    row = jnp.sum(onehot * (ends - padded)[None, :], axis=1) + rank
    row_token = jnp.zeros((N_ROWS_E,), jnp.int32).at[row].set(jnp.arange(N_ASSIGN, dtype=jnp.int32) // TOP_K)
    tile_start = jnp.arange(N_TILES_E, dtype=jnp.int32) * TM_E
    tile_expert = jnp.sum((tile_start[:, None] >= ends[None, :]).astype(jnp.int32), axis=1)
    tile_expert = jnp.minimum(tile_expert, N_EXPERTS - 1)
    real_end = jnp.sum(jnp.where(tile_expert[:, None] == jnp.arange(N_EXPERTS)[None, :], (ends - padded + counts)[None, :], 0), axis=1)
    part = TM_E // TILE_PARTS
    tile_valid = (jnp.clip(real_end - tile_start, 0, TM_E) + (part - 1)) // part
    return row, row_token, tile_expert + expert_base, tile_valid.astype(jnp.int32)


def _start_row_copies(src_hbm, rows_ref, first, stride, dst, sem, n, queues):
    def body(i, carry):
        for queue in range(queues):
            r = queues * i + queue
            src_row = rows_ref[first + stride * r]
            pltpu.make_async_copy(src_hbm.at[pl.ds(src_row, 1)], dst.at[pl.ds(r, 1)], sem).start(priority=queue)
        return carry

    lax.fori_loop(0, n // queues, body, 0)


def _wait_row_copies(src_hbm, dst, sem, n):
    pltpu.make_async_copy(src_hbm.at[pl.ds(0, n)], dst, sem).wait()


def _gather_rows_body(tok_ref, tv_ref, x_hbm, o_ref, buf, sem):
    t = pl.program_id(0)
    n_tiles = pl.num_programs(0)

    @pl.when(jnp.logical_and(t == 0, tv_ref[0] != 0))
    def _():
        _start_row_copies(x_hbm, tok_ref, 0, 1, buf.at[0], sem.at[0], TM_E, 1)

    @pl.when(jnp.logical_and(t + 1 < n_tiles, tv_ref[jnp.minimum(t + 1, n_tiles - 1)] != 0))
    def _():
        nxt = (t + 1) % 2
        _start_row_copies(x_hbm, tok_ref, (t + 1) * TM_E, 1, buf.at[nxt], sem.at[nxt], TM_E, 1)

    @pl.when(tv_ref[t] != 0)
    def _():
        slot = t % 2
        _wait_row_copies(x_hbm, buf.at[slot], sem.at[slot], TM_E)
        o_ref[...] = buf[slot].astype(BF16)

    @pl.when(tv_ref[t] == 0)
    def _():
        o_ref[...] = jnp.zeros(o_ref.shape, BF16)


def _gather_rows(x, row_token, tile_valid):
    d = x.shape[1]
    grid_spec = pltpu.PrefetchScalarGridSpec(
        num_scalar_prefetch=2,
        grid=(N_TILES_E,),
        in_specs=[pl.BlockSpec(memory_space=pl.ANY)],
        out_specs=pl.BlockSpec((TM_E, d), lambda t, tok, tv: (t, 0)),
        scratch_shapes=[pltpu.VMEM((2, TM_E, d), F32), pltpu.SemaphoreType.DMA((2,))])
    return pl.pallas_call(
        _gather_rows_body,
        out_shape=jax.ShapeDtypeStruct((N_ROWS_E, d), BF16),
        grid_spec=grid_spec,
        compiler_params=_params(1),
        name="gather_rows",
    )(row_token, tile_valid, x)


def _combine_body(row_ref, ys_hbm, gate_ref, res_ref, g_ref, b_ref, o_ref, ob_ref, buf, sem):
    t = pl.program_id(0)
    n_tiles = pl.num_programs(0)
    tm = TM_COMB

    def start(tile, slot):
        for k in range(TOP_K):
            _start_row_copies(ys_hbm, row_ref, TOP_K * tile * tm + k, TOP_K, buf.at[slot, k], sem.at[slot], tm,
                              N_DMA_QUEUES)

    @pl.when(t == 0)
    def _():
        start(0, 0)

    @pl.when(t + 1 < n_tiles)
    def _():
        start(t + 1, (t + 1) % 2)

    slot = t % 2
    for k in range(TOP_K):
        _wait_row_copies(ys_hbm, buf.at[slot, k], sem.at[slot], tm)
    y = gate_ref[:, 0:1] * buf[slot, 0] + gate_ref[:, 1:2] * buf[slot, 1]
    _residual_ln_store(y, res_ref, g_ref, b_ref, o_ref, ob_ref)


def _combine(ys, row, gate, res, ln):
    m, n = res.shape
    tm = TM_COMB
    rows = lambda i, r: (i, 0)
    grid_spec = pltpu.PrefetchScalarGridSpec(
        num_scalar_prefetch=1,
        grid=(m // tm,),
        in_specs=[pl.BlockSpec(memory_space=pl.ANY),
                  pl.BlockSpec((tm, TOP_K), rows),
                  pl.BlockSpec((tm, n), rows),
                  _layer_spec((1, n), ln[2]),
                  _layer_spec((1, n), ln[2])],
        out_specs=(pl.BlockSpec((tm, n), rows), pl.BlockSpec((tm, n), rows)),
        scratch_shapes=[pltpu.VMEM((2, TOP_K, tm, n), F32), pltpu.SemaphoreType.DMA((2,))])
    return pl.pallas_call(
        _combine_body,
        out_shape=(jax.ShapeDtypeStruct((m, n), F32), jax.ShapeDtypeStruct((m, n), BF16)),
        grid_spec=grid_spec,
        compiler_params=_params(1),
        name="combine",
    )(row, ys, gate, res, ln[0], ln[1])


def kernel(x_prompt, x_sample, state_a, state_b, state_d, cache_mem_k, cache_mem_v, mem_prompt, w_in, conv_a, pool_w, pool_scale, sg_ln_g, sg_ln_b, sg_w, sg_b, conv_d, cd_ln_g, cd_ln_b, w_out, w_q, w_k, w_v, w_o, ln_g, ln_b, dense_w_gate, dense_w_up, dense_w_down, moe_router, moe_w_gate, moe_w_up, moe_w_down):
    d, g = D_MODEL, D_GRP
    x = jnp.concatenate([x_prompt.reshape(N_PROMPT, d), x_sample.reshape(DEC_BATCH, d)], axis=0)
    xb = x.astype(BF16)
    mem_b = mem_prompt.reshape(BATCH * N_MEM, d).astype(BF16)
    n_tiles = N_TOK // TM
    all_valid = jnp.full((n_tiles,), TILE_PARTS, jnp.int32)
    spare = jnp.zeros((N_TOK, d), BF16)
    moe_wg = moe_w_gate.reshape(-1, d, D_FF)
    moe_wu = moe_w_up.reshape(-1, d, D_FF)
    moe_wd = moe_w_down.reshape(-1, D_FF, d)
    vecs = lambda v: v.reshape(v.shape[0], 1, v.shape[1])
    ps, lg, lb, dg, db = (vecs(v) for v in (pool_scale, sg_ln_g, sg_ln_b, cd_ln_g, cd_ln_b))
    sbt = sg_b.transpose(0, 2, 1)
    w0 = vecs(jnp.repeat(sg_w[:, :, 0, 0], CHUNK, axis=1))
    b0 = vecs(jnp.repeat(sg_b[:, :, 0], CHUNK, axis=1))
    n_norms = ln_g.shape[1]
    ln_all = (ln_g.reshape(DEPTH * n_norms, 1, d), ln_b.reshape(DEPTH * n_norms, 1, d))

    sa_p, sb_p, sd_p, mk_p, mv_p, sa_s, sb_s, sd_s, sc_s = [], [], [], [], [], [], [], [], []
    for l in range(DEPTH):
        ln = lambda i: ln_all + (l * n_norms + i,)
        h = _matmul(xb, w_in, l, F32, TM, TN, "w_in")
        y_p, ha, hb, hd = _mix_prompt(spare, h, l, conv_a, pool_w, ps, lg, lb, sg_w, sbt, conv_d, dg, db)
        y_mix, ta, tb, td, tc = _mix_sample(
            y_p, h, state_a[l].reshape(DEC_BATCH, -1), state_b[l].reshape(DEC_BATCH, -1), state_d[l].reshape(DEC_BATCH, -1),
            l, conv_a, pool_w, ps, lg, lb, w0, b0, conv_d, dg, db)
        x, xb = _mm_ln_resident(y_mix, w_out, l, x, ln(0), "w_out_ln")

        q = _matmul(xb, w_q, l, BF16, TM, TN, "w_q")
        mk = _matmul(mem_b, w_k, l, F32, 512, TN, "w_k")
        mv = _matmul(mem_b, w_v, l, F32, 512, TN, "w_v")
        o_p = _attn_prompt(y_mix, q, mk, mv)
        o_s = _attn_sample(q[N_PROMPT:].astype(F32), cache_mem_k, cache_mem_v, l)
        o = _paste_rows(o_p, o_s.astype(BF16))
        j = l // 2
        is_moe = l % 2 == 1
        x, xb = _mm_ln_resident(o, w_o, l, x, ln(1), "w_o_ln")
        spare = o

        if not is_moe:
            tabs = _run_tables(jnp.full((n_tiles,), j, jnp.int32), all_valid)
            hmid = _ffn_up(xb, dense_w_gate, dense_w_up, tabs, TM)
            y = _ffn_down(hmid, dense_w_down, tabs, TM)
            x, xb = _add_ln(y, x, ln(2))
        else:
            idx, gate = _router(x, moe_router[j])
            row, row_token, tile_expert, tile_valid = _route_tables(idx, j * N_EXPERTS)
            tabs = _run_tables(tile_expert, tile_valid)
            xs = _gather_rows(x, row_token, tile_valid)
            hmid = _ffn_up(xs, moe_wg, moe_wu, tabs, TM_E)
            ys = _ffn_down(hmid, moe_wd, tabs, TM_E)
            x, xb = _combine(ys, row, gate, x, ln(2))

        sa_p.append(ha); sb_p.append(hb); sd_p.append(hd)
        mk_p.append(mk.reshape(BATCH, N_MEM, N_HEADS_X, D_HEAD_X)); mv_p.append(mv.reshape(BATCH, N_MEM, N_HEADS_X, D_HEAD_X))
        sa_s.append(ta.reshape(DEC_BATCH, CONV_A - 1, g)); sb_s.append(tb.reshape(DEC_BATCH, POOL_HIST, g))
        sd_s.append(td.reshape(DEC_BATCH, CONV_D - 1, g)); sc_s.append(tc.reshape(DEC_BATCH, 1, g))

    return (x[:N_PROMPT].reshape(BATCH, SEQ, d), x[N_PROMPT:].reshape(DEC_BATCH, 1, d),
            jnp.stack(sa_p), jnp.stack(sb_p), jnp.stack(sd_p), jnp.stack(mk_p), jnp.stack(mv_p),
            jnp.stack(sa_s), jnp.stack(sb_s), jnp.stack(sd_s), jnp.stack(sc_s))
```

```python
import functools

import jax
import jax.numpy as jnp
from jax import lax
from jax.experimental import pallas as pl
from jax.experimental.pallas import tpu as pltpu

F32 = jnp.float32
BF16 = jnp.bfloat16

D_MODEL = 2048
BATCH = 4
SEQ = 2048
DEPTH = 4
DEC_BATCH = 128
PAST_LEN = 16384
D_GRP = 512
D_IN = 8 * D_GRP
CONV_A = 3
POOL_WINDOWS = (2, 4, 8, 16)
D_POOL = 128
POOL_HIST = 15
CHUNK = 128
N_HEADS_C = 4
CONV_D = 31
N_MEM = 256
N_HEADS_X = 4
D_HEAD_X = 512
D_FF = 5632
N_EXPERTS = 8
ALPHA = (2 * DEPTH) ** 0.25
LN_EPS = 1e-5
ATTN_SCALE = D_HEAD_X ** -0.5

N_PROMPT = BATCH * SEQ
N_TOK = N_PROMPT + DEC_BATCH

VMEM_LIMIT_BYTES = 52 * 1024 * 1024

TM = 640
TM_LN = 320
TN = 1024
TF = 512
LN_ROW_GROUPS = 2
LANES = 128
TN_DOWN = 512
T_MIX = 256
SUBLANES = 8
ROWS_D = 64
LANES_D = 256
ROWS_LN = 32
HALO_A, HALO_B, HALO_D = 8, 16, 32
S_MIX = 32
TQ = 1024
BS_ATT = 4
TM_E = 512
TM_COMB = 640
TOP_K = 2
TILE_PARTS = 4
N_DMA_QUEUES = 2
N_ASSIGN = TOP_K * N_TOK
N_TILES_E = -(-(N_ASSIGN + N_EXPERTS * (TM_E - 1)) // TM_E)
N_ROWS_E = N_TILES_E * TM_E


def _params(n_axes):
    return pltpu.CompilerParams(dimension_semantics=("arbitrary",) * n_axes,
                                vmem_limit_bytes=VMEM_LIMIT_BYTES)


def _layer_spec(shape, *index):
    return pl.BlockSpec((None,) * len(index) + tuple(shape), lambda *_: tuple(index) + (0,) * len(shape))


def _layer_norm(x, g, b):
    mu = jnp.mean(x, axis=-1, keepdims=True)
    xc = x - mu
    var = jnp.mean(xc * xc, axis=-1, keepdims=True)
    return xc * lax.rsqrt(var + LN_EPS) * g + b


def _sigmoid(x):
    return 1.0 / (1.0 + jnp.exp(-x))


def _mm_body(x_ref, w_ref, o_ref, wb_ref):
    @pl.when(pl.program_id(1) == 0)
    def _():
        wb_ref[...] = w_ref[...].astype(BF16)

    o_ref[...] = jnp.dot(x_ref[...], wb_ref[...], preferred_element_type=F32).astype(o_ref.dtype)


def _matmul(x, w, layer, out_dtype, tm, tn, name):
    m, k = x.shape
    n = w.shape[2]
    return pl.pallas_call(
        _mm_body,
        out_shape=jax.ShapeDtypeStruct((m, n), out_dtype),
        grid=(n // tn, m // tm),
        in_specs=[pl.BlockSpec((tm, k), lambda j, i: (i, 0)),
                  pl.BlockSpec((None, k, tn), lambda j, i: (layer, 0, j))],
        out_specs=pl.BlockSpec((tm, tn), lambda j, i: (i, j)),
        scratch_shapes=[pltpu.VMEM((k, tn), BF16)],
        compiler_params=_params(2),
        name=name,
    )(x, w)


def _residual_ln_store(acc, res_ref, g_ref, b_ref, o_ref, ob_ref):
    y = _layer_norm(ALPHA * res_ref[...] + acc, g_ref[...], b_ref[...])
    o_ref[...] = y
    ob_ref[...] = y.astype(BF16)


def _mm_ln_resident_body(a_ref, w_ref, res_ref, g_ref, b_ref, o_ref, ob_ref, wb_ref):
    @pl.when(pl.program_id(0) == 0)
    def _():
        wb_ref[...] = w_ref[...].astype(BF16)

    rows_per_group = a_ref.shape[0] // LN_ROW_GROUPS
    for s in range(LN_ROW_GROUPS):
        rows = slice(s * rows_per_group, (s + 1) * rows_per_group)
        acc = jnp.dot(a_ref[rows, :], wb_ref[...], preferred_element_type=F32)
        y = _layer_norm(ALPHA * res_ref[rows, :] + acc, g_ref[...], b_ref[...])
        o_ref[rows, :] = y
        ob_ref[rows, :] = y.astype(BF16)


def _mm_ln_resident(a, w, layer, res, ln, name):
    m, k = a.shape
    n = w.shape[2]
    tm = TM_LN
    row = lambda i: (i, 0)
    return pl.pallas_call(
        _mm_ln_resident_body,
        out_shape=(jax.ShapeDtypeStruct((m, n), F32), jax.ShapeDtypeStruct((m, n), BF16)),
        grid=(m // tm,),
        in_specs=[pl.BlockSpec((tm, k), row),
                  pl.BlockSpec((None, k, n), lambda i: (layer, 0, 0), pipeline_mode=pl.Buffered(1)),
                  pl.BlockSpec((tm, n), row),
                  _layer_spec((1, n), ln[2]),
                  _layer_spec((1, n), ln[2])],
        out_specs=(pl.BlockSpec((tm, n), row), pl.BlockSpec((tm, n), row)),
        scratch_shapes=[pltpu.VMEM((k, n), BF16)],
        compiler_params=_params(1),
        name=name,
    )(a, w, res, ln[0], ln[1])


def _add_ln_body(y_ref, res_ref, g_ref, b_ref, o_ref, ob_ref):
    _residual_ln_store(y_ref[...], res_ref, g_ref, b_ref, o_ref, ob_ref)


def _add_ln(y, res, ln):
    m, n = res.shape
    tm = TM_LN
    row = lambda i: (i, 0)
    return pl.pallas_call(
        _add_ln_body,
        out_shape=(jax.ShapeDtypeStruct((m, n), F32), jax.ShapeDtypeStruct((m, n), BF16)),
        grid=(m // tm,),
        in_specs=[pl.BlockSpec((tm, n), row), pl.BlockSpec((tm, n), row),
                  _layer_spec((1, n), ln[2]), _layer_spec((1, n), ln[2])],
        out_specs=(pl.BlockSpec((tm, n), row), pl.BlockSpec((tm, n), row)),
        compiler_params=_params(1),
        name="add_ln",
    )(y, res, ln[0], ln[1])


def _mix_prompt_body(y_all_ref, h_ref, ca_ref, pw_ref, ps_ref, lg_ref, lb_ref, sw_ref, sbt_ref, cd_ref, dg_ref, db_ref,
                     y_ref, sa_ref, sb_ref, sd_ref, ea_ref, eb_ref, ed_ref, vn_ref, cv_ref, *, n_steps):
    del y_all_ref
    t = T_MIX
    g = D_GRP
    s = pl.program_id(1)

    @pl.when(s == 0)
    def _():
        ea_ref[0:HALO_A, :] = jnp.zeros((HALO_A, g), F32)
        eb_ref[0:HALO_B, :] = jnp.zeros((HALO_B, g), F32)
        ed_ref[0:HALO_D, :] = jnp.zeros((HALO_D, g), F32)

    e = h_ref[:, g:2 * g] * h_ref[:, 2 * g:3 * g]
    ea_ref[HALO_A:HALO_A + t, :] = e
    conv_a = (ca_ref[2:3, :] * e + ca_ref[1:2, :] * ea_ref[HALO_A - 1:HALO_A - 1 + t, :]
              + ca_ref[0:1, :] * ea_ref[HALO_A - 2:HALO_A - 2 + t, :])
    y_ref[:, 0:g] = (h_ref[:, 0:g] * conv_a).astype(BF16)

    eb_ref[HALO_B:HALO_B + t, :] = h_ref[:, 3 * g:4 * g]
    pos = s * t + lax.broadcasted_iota(jnp.int32, (t, 1), 0)
    for gi, w in enumerate(POOL_WINDOWS):
        lo = gi * D_POOL
        cur = h_ref[:, 3 * g + lo:3 * g + lo + D_POOL]
        win = cur
        for k in range(1, w):
            win = win + eb_ref[HALO_B - k:HALO_B - k + t, lo:lo + D_POOL]
        cnt = jnp.minimum(pos + 1, w).astype(F32)
        pooled = win / cnt - cur
        yb = jnp.dot(pooled.astype(BF16), pw_ref[gi].astype(BF16), preferred_element_type=F32)
        y_ref[:, g + lo:g + lo + D_POOL] = (yb * ps_ref[:, lo:lo + D_POOL]).astype(BF16)

    vn_ref[...] = _layer_norm(h_ref[:, 5 * g:6 * g], lg_ref[...], lb_ref[...])
    n_chunks = t // CHUNK
    rows = lax.broadcasted_iota(jnp.int32, (CHUNK, CHUNK), 0)
    cols = lax.broadcasted_iota(jnp.int32, (CHUNK, CHUNK), 1)
    for hh in range(N_HEADS_C):
        lo = hh * CHUNK
        w_tril = jnp.where(rows >= cols, sw_ref[hh], 0.0).astype(BF16)
        rhs = jnp.concatenate([vn_ref[c * CHUNK:(c + 1) * CHUNK, lo:lo + CHUNK] for c in range(n_chunks)], axis=1)
        mixed = jnp.dot(w_tril, rhs.astype(BF16), preferred_element_type=F32)
        bias = sbt_ref[:, hh:hh + 1]
        for c in range(n_chunks):
            u = h_ref[c * CHUNK:(c + 1) * CHUNK, 4 * g + lo:4 * g + lo + CHUNK]
            y_ref[c * CHUNK:(c + 1) * CHUNK, 2 * g + lo:2 * g + lo + CHUNK] = (
                u * (mixed[:, c * CHUNK:(c + 1) * CHUNK] + bias)).astype(BF16)

    ed_ref[HALO_D:HALO_D + t, :] = h_ref[:, 6 * g:7 * g] * _sigmoid(h_ref[:, 7 * g:8 * g])
    for rb in range(t // ROWS_D):
        r0 = HALO_D + rb * ROWS_D
        for lb in range(g // LANES_D):
            ls = slice(lb * LANES_D, (lb + 1) * LANES_D)
            acc = None
            for b in range(SUBLANES):
                z_b = None
                for a in range(-(-CONV_D // SUBLANES)):
                    shift = SUBLANES * a + b
                    if shift < CONV_D:
                        lo = r0 - SUBLANES * (a + 1)
                        term = cd_ref[CONV_D - 1 - shift:CONV_D - shift, ls] * ed_ref[lo:lo + ROWS_D + SUBLANES, ls]
                        z_b = term if z_b is None else z_b + term
                part = z_b[SUBLANES - b:SUBLANES - b + ROWS_D, :]
                acc = part if acc is None else acc + part
            cv_ref[rb * ROWS_D:(rb + 1) * ROWS_D, ls] = acc
    for rb in range(t // ROWS_LN):
        rows = slice(rb * ROWS_LN, (rb + 1) * ROWS_LN)
        z = _layer_norm(cv_ref[rows, :], dg_ref[...], db_ref[...])
        y_ref[rows, 3 * g:4 * g] = (z * _sigmoid(z)).astype(BF16)

    @pl.when(s == n_steps - 1)
    def _():
        sa_ref[0] = ea_ref[HALO_A + t - (CONV_A - 1):HALO_A + t, :]
        sb_ref[0] = eb_ref[HALO_B + t - POOL_HIST:HALO_B + t, :]
        sd_ref[0] = ed_ref[HALO_D + t - (CONV_D - 1):HALO_D + t, :]

    ea_ref[0:HALO_A, :] = ea_ref[t:t + HALO_A, :]
    eb_ref[0:HALO_B, :] = eb_ref[t:t + HALO_B, :]
    ed_ref[0:HALO_D, :] = ed_ref[t:t + HALO_D, :]


def _mix_prompt(y_all, h, layer, ca, pw, ps, lg, lb, sw, sbt, cd, dg, db):
    n_steps = SEQ // T_MIX
    g = D_GRP
    c2 = c3 = layer
    st = lambda b, s: (b, 0, 0)
    return pl.pallas_call(
        functools.partial(_mix_prompt_body, n_steps=n_steps),
        out_shape=(jax.ShapeDtypeStruct((N_TOK, D_MODEL), BF16),
                   jax.ShapeDtypeStruct((BATCH, CONV_A - 1, g), F32),
                   jax.ShapeDtypeStruct((BATCH, POOL_HIST, g), F32),
                   jax.ShapeDtypeStruct((BATCH, CONV_D - 1, g), F32)),
        grid=(BATCH, n_steps),
        input_output_aliases={0: 0},
        in_specs=[pl.BlockSpec(memory_space=pl.ANY),
                  pl.BlockSpec((T_MIX, D_IN), lambda b, s: (b * n_steps + s, 0)),
                  _layer_spec((CONV_A, g), c2),
                  _layer_spec((len(POOL_WINDOWS), D_POOL, D_POOL), c3),
                  _layer_spec((1, g), c2),
                  _layer_spec((1, g), c2),
                  _layer_spec((1, g), c2),
                  _layer_spec((N_HEADS_C, CHUNK, CHUNK), c3),
                  _layer_spec((CHUNK, N_HEADS_C), c2),
                  _layer_spec((CONV_D, g), c2),
                  _layer_spec((1, g), c2),
                  _layer_spec((1, g), c2)],
        out_specs=(pl.BlockSpec((T_MIX, D_MODEL), lambda b, s: (b * n_steps + s, 0)),
                   pl.BlockSpec((1, CONV_A - 1, g), st),
                   pl.BlockSpec((1, POOL_HIST, g), st),
                   pl.BlockSpec((1, CONV_D - 1, g), st)),
        scratch_shapes=[pltpu.VMEM((HALO_A + T_MIX, g), F32),
                        pltpu.VMEM((HALO_B + T_MIX, g), F32),
                        pltpu.VMEM((HALO_D + T_MIX, g), F32),
                        pltpu.VMEM((T_MIX, g), F32),
                        pltpu.VMEM((T_MIX, g), F32)],
        compiler_params=_params(2),
        name="mix_prompt",
    )(y_all, h, ca, pw, ps, lg, lb, sw, sbt, cd, dg, db)


def _mix_sample_body(y_all_ref, h_ref, sa_ref, sb_ref, sd_ref, ca_ref, pw_ref, ps_ref, lg_ref, lb_ref, w0_ref, b0_ref,
                     cd_ref, dg_ref, db_ref, y_ref, na_ref, nb_ref, nd_ref, vn_ref):
    del y_all_ref
    g = D_GRP

    e = h_ref[:, g:2 * g] * h_ref[:, 2 * g:3 * g]
    conv_a = ca_ref[2:3, :] * e + ca_ref[1:2, :] * sa_ref[:, g:2 * g] + ca_ref[0:1, :] * sa_ref[:, 0:g]
    y_ref[:, 0:g] = (h_ref[:, 0:g] * conv_a).astype(BF16)
    na_ref[:, 0:g] = sa_ref[:, g:2 * g]
    na_ref[:, g:2 * g] = e

    for gi, w in enumerate(POOL_WINDOWS):
        lo = gi * D_POOL
        cur = h_ref[:, 3 * g + lo:3 * g + lo + D_POOL]
        win = cur
        for k in range(1, w):
            col = (POOL_HIST - k) * g + lo
            win = win + sb_ref[:, col:col + D_POOL]
        cnt = float(min(PAST_LEN + 1, w))
        pooled = win / cnt - cur
        yb = jnp.dot(pooled.astype(BF16), pw_ref[gi].astype(BF16), preferred_element_type=F32)
        y_ref[:, g + lo:g + lo + D_POOL] = (yb * ps_ref[:, lo:lo + D_POOL]).astype(BF16)
    nb_ref[:, 0:(POOL_HIST - 1) * g] = sb_ref[:, g:POOL_HIST * g]
    nb_ref[:, (POOL_HIST - 1) * g:POOL_HIST * g] = h_ref[:, 3 * g:4 * g]

    vn = _layer_norm(h_ref[:, 5 * g:6 * g], lg_ref[...], lb_ref[...])
    vn_ref[...] = vn
    y_ref[:, 2 * g:3 * g] = (h_ref[:, 4 * g:5 * g] * (w0_ref[...] * vn + b0_ref[...])).astype(BF16)

    glu = h_ref[:, 6 * g:7 * g] * _sigmoid(h_ref[:, 7 * g:8 * g])
    acc = cd_ref[CONV_D - 1:CONV_D, :] * glu
    for k in range(CONV_D - 1):
        acc = acc + cd_ref[k:k + 1, :] * sd_ref[:, k * g:(k + 1) * g]
    z = _layer_norm(acc, dg_ref[...], db_ref[...])
    y_ref[:, 3 * g:4 * g] = (z * _sigmoid(z)).astype(BF16)
    nd_ref[:, 0:(CONV_D - 2) * g] = sd_ref[:, g:(CONV_D - 1) * g]
    nd_ref[:, (CONV_D - 2) * g:(CONV_D - 1) * g] = glu


def _mix_sample(y_all, h, sa, sb, sd, layer, ca, pw, ps, lg, lb, w0, b0, cd, dg, db):
    g = D_GRP
    n_steps = DEC_BATCH // S_MIX
    first = N_PROMPT // S_MIX
    row = lambda i: (i, 0)
    c2 = c3 = layer
    wa, wb, wd = (CONV_A - 1) * g, POOL_HIST * g, (CONV_D - 1) * g
    return pl.pallas_call(
        _mix_sample_body,
        out_shape=(jax.ShapeDtypeStruct((N_TOK, D_MODEL), BF16),
                   jax.ShapeDtypeStruct((DEC_BATCH, wa), F32),
                   jax.ShapeDtypeStruct((DEC_BATCH, wb), F32),
                   jax.ShapeDtypeStruct((DEC_BATCH, wd), F32),
                   jax.ShapeDtypeStruct((DEC_BATCH, g), F32)),
        grid=(n_steps,),
        input_output_aliases={0: 0},
        in_specs=[pl.BlockSpec(memory_space=pl.ANY),
                  pl.BlockSpec((S_MIX, D_IN), lambda i: (first + i, 0)),
                  pl.BlockSpec((S_MIX, wa), row),
                  pl.BlockSpec((S_MIX, wb), row),
                  pl.BlockSpec((S_MIX, wd), row),
                  _layer_spec((CONV_A, g), c2),
                  _layer_spec((len(POOL_WINDOWS), D_POOL, D_POOL), c3),
                  _layer_spec((1, g), c2),
                  _layer_spec((1, g), c2),
                  _layer_spec((1, g), c2),
                  _layer_spec((1, g), c2),
                  _layer_spec((1, g), c2),
                  _layer_spec((CONV_D, g), c2),
                  _layer_spec((1, g), c2),
                  _layer_spec((1, g), c2)],
        out_specs=(pl.BlockSpec((S_MIX, D_MODEL), lambda i: (first + i, 0)),
                   pl.BlockSpec((S_MIX, wa), row),
                   pl.BlockSpec((S_MIX, wb), row),
                   pl.BlockSpec((S_MIX, wd), row),
                   pl.BlockSpec((S_MIX, g), row)),
        compiler_params=_params(1),
        name="mix_sample",
    )(y_all, h, sa, sb, sd, ca, pw, ps, lg, lb, w0, b0, cd, dg, db)


def _attn_prompt_body(o_all_ref, q_ref, k_ref, v_ref, o_ref):
    del o_all_ref
    for hh in range(N_HEADS_X):
        sl = slice(hh * D_HEAD_X, (hh + 1) * D_HEAD_X)
        kh = k_ref[:, sl].astype(BF16)
        vh = v_ref[:, sl].astype(BF16)
        s = lax.dot_general(q_ref[:, sl], kh, (((1,), (1,)), ((), ())), preferred_element_type=F32) * ATTN_SCALE
        p = jnp.exp(s - jnp.max(s, axis=-1, keepdims=True))
        p = p / jnp.sum(p, axis=-1, keepdims=True)
        o_ref[:, sl] = jnp.dot(p.astype(BF16), vh, preferred_element_type=F32).astype(BF16)


def _attn_prompt(o_all, q, mk, mv):
    n_q = SEQ // TQ
    kv = pl.BlockSpec((N_MEM, D_MODEL), lambda b, i: (b, 0))
    qo = pl.BlockSpec((TQ, D_MODEL), lambda b, i: (b * n_q + i, 0))
    return pl.pallas_call(
        _attn_prompt_body,
        out_shape=jax.ShapeDtypeStruct((N_TOK, D_MODEL), BF16),
        grid=(BATCH, n_q),
        input_output_aliases={0: 0},
        in_specs=[pl.BlockSpec(memory_space=pl.ANY), qo, kv, kv],
        out_specs=qo,
        compiler_params=_params(2),
        name="attn_prompt",
    )(o_all, q, mk, mv)


def _paste_rows_body(all_ref, rows_ref, o_ref):
    del all_ref
    o_ref[...] = rows_ref[...]


def _paste_rows(x_all, rows):
    n, d = rows.shape
    return pl.pallas_call(
        _paste_rows_body,
        out_shape=jax.ShapeDtypeStruct(x_all.shape, x_all.dtype),
        grid=(1,),
        input_output_aliases={0: 0},
        in_specs=[pl.BlockSpec(memory_space=pl.ANY), pl.BlockSpec((n, d), lambda i: (0, 0))],
        out_specs=pl.BlockSpec((n, d), lambda i: (N_PROMPT // n, 0)),
        compiler_params=_params(1),
        name="paste_rows",
    )(x_all, rows)


N_LANE_TILES_X = D_HEAD_X // LANES
HALF = N_HEADS_X


def _attn_sample_body(q_ref, *refs):
    nc = N_LANE_TILES_X
    packed = (N_MEM // 2, 2 * N_HEADS_X, LANES)
    k_refs, v_refs, o_ref = refs[:nc], refs[nc:2 * nc], refs[2 * nc]
    for j in range(BS_ATT):
        t = k_refs[0][j].reshape(packed) * q_ref[j, 0][None]
        for c in range(1, nc):
            t = t + k_refs[c][j].reshape(packed) * q_ref[j, c][None]
        s = jnp.sum(t, axis=-1, keepdims=True) * ATTN_SCALE
        m = jnp.max(s, axis=0, keepdims=True)
        m = jnp.maximum(m, pltpu.roll(m, HALF, 1))
        p = jnp.exp(s - m)
        l = jnp.sum(p, axis=0, keepdims=True)
        p = p / (l + pltpu.roll(l, HALF, 1))
        for c in range(nc):
            acc = jnp.sum(p * v_refs[c][j].reshape(packed), axis=0)
            o_ref[j, c] = acc + pltpu.roll(acc, HALF, 0)


def _attn_sample(q, ck, cv, layer):
    nc = N_LANE_TILES_X
    q2 = q.reshape(DEC_BATCH, N_HEADS_X, nc, LANES).transpose(0, 2, 1, 3)
    q2 = jnp.concatenate([q2, q2], axis=2)
    qo = pl.BlockSpec((BS_ATT, nc, 2 * N_HEADS_X, LANES), lambda i: (i, 0, 0, 0))
    kv = [pl.BlockSpec((None, BS_ATT, N_MEM, N_HEADS_X, LANES), lambda i, c=c: (layer, i, 0, 0, c)) for c in range(nc)]
    o2 = pl.pallas_call(
        _attn_sample_body,
        out_shape=jax.ShapeDtypeStruct((DEC_BATCH, nc, 2 * N_HEADS_X, LANES), F32),
        grid=(DEC_BATCH // BS_ATT,),
        in_specs=[qo] + kv + kv,
        out_specs=qo,
        compiler_params=_params(1),
        name="attn_sample",
    )(q2, *([ck] * nc), *([cv] * nc))
    return o2[:, :, :N_HEADS_X, :].transpose(0, 2, 1, 3).reshape(DEC_BATCH, D_MODEL)


def _run_tables(tile_expert, tile_valid):
    n = tile_expert.shape[0]
    prev = jnp.concatenate([tile_expert[:1] - 1, tile_expert[:-1]])
    first = (tile_expert != prev).astype(jnp.int32)
    run = jnp.cumsum(first) - 1
    n_runs = run[-1:] + 1
    runs = jnp.arange(n, dtype=jnp.int32)
    run_expert = jnp.sum(jnp.where((run[None, :] == runs[:, None]) & (first[None, :] == 1), tile_expert[None, :], 0), axis=1)
    next_run = jnp.where(run + 1 < n_runs, run + 1, 0)
    next_expert = jnp.sum(jnp.where(next_run[:, None] == runs[None, :], run_expert[None, :], 0), axis=1)
    return tile_expert, tile_valid, first, run, next_expert.astype(jnp.int32), n_runs


def _fetch_run_weights(tabs, w_hbm, stage, wb, sem, width):
    te_ref, _, first_ref, run_ref, next_ref, nruns_ref = tabs
    col, t = pl.program_id(0), pl.program_id(1)
    n_runs = nruns_ref[0]
    seq = col * n_runs + run_ref[t]

    def copies(expert, col_block, slot):
        lanes = pl.ds(pl.multiple_of(col_block * width, width), width)
        return [pltpu.make_async_copy(w.at[expert, :, lanes], st.at[slot], sem.at[slot, i])
                for i, (w, st) in enumerate(zip(w_hbm, stage))]

    @pl.when(first_ref[t] == 1)
    def _():
        slot = seq % 2

        @pl.when(seq == 0)
        def _():
            for c in copies(te_ref[t], col, slot):
                c.start()

        for c in copies(te_ref[t], col, slot):
            c.wait()
        for st, dst in zip(stage, wb):
            dst[...] = st[slot].astype(BF16)

        last_run = run_ref[t] + 1 == n_runs

        @pl.when(jnp.logical_or(jnp.logical_not(last_run), col + 1 < pl.num_programs(0)))
        def _():
            for c in copies(next_ref[t], jnp.where(last_run, col + 1, col), 1 - slot):
                c.start()


def _for_real_rows(n_parts, o_ref, compute):
    tile_rows = o_ref.shape[0]
    rows_per_part = tile_rows // TILE_PARTS
    for count in range(TILE_PARTS + 1):
        real = count * rows_per_part

        @pl.when(n_parts == count)
        def _():
            if real > 0:
                compute(slice(0, real))
            if real < tile_rows:
                o_ref[real:tile_rows, :] = jnp.zeros((tile_rows - real, o_ref.shape[1]), o_ref.dtype)


def _ffn_up_body(te_ref, tv_ref, first_ref, run_ref, next_ref, nruns_ref, x_ref, wg_hbm, wu_hbm, o_ref,
                 sg_ref, su_ref, wgb_ref, wub_ref, sem):
    t = pl.program_id(1)
    _fetch_run_weights((te_ref, tv_ref, first_ref, run_ref, next_ref, nruns_ref), (wg_hbm, wu_hbm),
                       (sg_ref, su_ref), (wgb_ref, wub_ref), sem, TF)

    def compute(rows):
        x = x_ref[rows, :]
        gate = jnp.dot(x, wgb_ref[...], preferred_element_type=F32)
        up = jnp.dot(x, wub_ref[...], preferred_element_type=F32)
        o_ref[rows, :] = (gate * _sigmoid(gate) * up).astype(BF16)

    _for_real_rows(tv_ref[t], o_ref, compute)


def _ffn_up(x, wg, wu, tabs, tm):
    n_tiles = tabs[0].shape[0]
    k = x.shape[1]
    grid_spec = pltpu.PrefetchScalarGridSpec(
        num_scalar_prefetch=len(tabs),
        grid=(D_FF // TF, n_tiles),
        in_specs=[pl.BlockSpec((tm, k), lambda f, t, *_: (t, 0)),
                  pl.BlockSpec(memory_space=pl.ANY),
                  pl.BlockSpec(memory_space=pl.ANY)],
        out_specs=pl.BlockSpec((tm, TF), lambda f, t, *_: (t, f)),
        scratch_shapes=[pltpu.VMEM((2, k, TF), F32), pltpu.VMEM((2, k, TF), F32),
                        pltpu.VMEM((k, TF), BF16), pltpu.VMEM((k, TF), BF16),
                        pltpu.SemaphoreType.DMA((2, 2))])
    return pl.pallas_call(
        _ffn_up_body,
        out_shape=jax.ShapeDtypeStruct((n_tiles * tm, D_FF), BF16),
        grid_spec=grid_spec,
        compiler_params=_params(2),
        name="ffn_up",
    )(*tabs, x, wg, wu)


def _ffn_down_body(te_ref, tv_ref, first_ref, run_ref, next_ref, nruns_ref, h_ref, wd_hbm, o_ref,
                   sd_ref, wdb_ref, sem):
    t = pl.program_id(1)
    _fetch_run_weights((te_ref, tv_ref, first_ref, run_ref, next_ref, nruns_ref), (wd_hbm,),
                       (sd_ref,), (wdb_ref,), sem, TN_DOWN)

    def compute(rows):
        o_ref[rows, :] = jnp.dot(h_ref[rows, :], wdb_ref[...], preferred_element_type=F32)

    _for_real_rows(tv_ref[t], o_ref, compute)


def _ffn_down(hmid, wd, tabs, tm):
    n_tiles = tabs[0].shape[0]
    n = wd.shape[2]
    grid_spec = pltpu.PrefetchScalarGridSpec(
        num_scalar_prefetch=len(tabs),
        grid=(n // TN_DOWN, n_tiles),
        in_specs=[pl.BlockSpec((tm, D_FF), lambda j, t, *_: (t, 0)),
                  pl.BlockSpec(memory_space=pl.ANY)],
        out_specs=pl.BlockSpec((tm, TN_DOWN), lambda j, t, *_: (t, j)),
        scratch_shapes=[pltpu.VMEM((2, D_FF, TN_DOWN), F32), pltpu.VMEM((D_FF, TN_DOWN), BF16),
                        pltpu.SemaphoreType.DMA((2, 1))])
    return pl.pallas_call(
        _ffn_down_body,
        out_shape=jax.ShapeDtypeStruct((n_tiles * tm, n), F32),
        grid_spec=grid_spec,
        compiler_params=_params(2),
        name="ffn_down",
    )(*tabs, hmid, wd)


def _router_body(x_ref, r_ref, idx_ref, gate_ref):
    logits = jnp.dot(x_ref[...], r_ref[...], preferred_element_type=F32, precision=lax.Precision.HIGHEST)
    lane = lax.broadcasted_iota(jnp.int32, logits.shape, 1).astype(F32)
    n = float(N_EXPERTS)
    m1 = jnp.max(logits, axis=-1, keepdims=True)
    i1 = jnp.min(jnp.where(logits == m1, lane, n), axis=-1, keepdims=True)
    rest = jnp.where(lane == i1, -jnp.inf, logits)
    m2 = jnp.max(rest, axis=-1, keepdims=True)
    i2 = jnp.min(jnp.where(rest == m2, lane, n), axis=-1, keepdims=True)
    e2 = jnp.exp(m2 - m1)
    g1 = 1.0 / (1.0 + e2)
    first = lax.broadcasted_iota(jnp.int32, idx_ref.shape, 1) == 0
    idx_ref[...] = jnp.where(first, i1, i2).astype(jnp.int32)
    gate_ref[...] = jnp.where(first, g1, e2 * g1)


def _router(x, r):
    m, k = x.shape
    out = pl.BlockSpec((TM, TOP_K), lambda i: (i, 0))
    return pl.pallas_call(
        _router_body,
        out_shape=(jax.ShapeDtypeStruct((m, TOP_K), jnp.int32), jax.ShapeDtypeStruct((m, TOP_K), F32)),
        grid=(m // TM,),
        in_specs=[pl.BlockSpec((TM, k), lambda i: (i, 0)),
                  pl.BlockSpec((k, N_EXPERTS), lambda i: (0, 0))],
        out_specs=(out, out),
        compiler_params=_params(1),
        name="router",
    )(x, r)


def _route_tables(idx, expert_base):
    a = idx.reshape(-1)
    onehot = (a[:, None] == jnp.arange(N_EXPERTS, dtype=jnp.int32)[None, :]).astype(jnp.int32)
    csum = jnp.cumsum(onehot, axis=0)
    rank = jnp.sum(onehot * csum, axis=1) - 1
    counts = csum[-1]
    padded = (counts + (TM_E - 1)) // TM_E * TM_E
    ends = jnp.cumsum(padded)
    row = jnp.sum(onehot * (ends - padded)[None, :], axis=1) + rank
    row_token = jnp.zeros((N_ROWS_E,), jnp.int32).at[row].set(jnp.arange(N_ASSIGN, dtype=jnp.int32) // TOP_K)
    tile_start = jnp.arange(N_TILES_E, dtype=jnp.int32) * TM_E
    tile_expert = jnp.sum((tile_start[:, None] >= ends[None, :]).astype(jnp.int32), axis=1)
    tile_expert = jnp.minimum(tile_expert, N_EXPERTS - 1)
    real_end = jnp.sum(jnp.where(tile_expert[:, None] == jnp.arange(N_EXPERTS)[None, :], (ends - padded + counts)[None, :], 0), axis=1)
    part = TM_E // TILE_PARTS
    tile_valid = (jnp.clip(real_end - tile_start, 0, TM_E) + (part - 1)) // part
    return row, row_token, tile_expert + expert_base, tile_valid.astype(jnp.int32)


def _start_row_copies(src_hbm, rows_ref, first, stride, dst, sem, n, queues):
    def body(i, carry):
        for queue in range(queues):
            r = queues * i + queue
            src_row = rows_ref[first + stride * r]
            pltpu.make_async_copy(src_hbm.at[pl.ds(src_row, 1)], dst.at[pl.ds(r, 1)], sem).start(priority=queue)
        return carry

    lax.fori_loop(0, n // queues, body, 0)


def _wait_row_copies(src_hbm, dst, sem, n):
    pltpu.make_async_copy(src_hbm.at[pl.ds(0, n)], dst, sem).wait()


def _gather_rows_body(tok_ref, tv_ref, x_hbm, o_ref, buf, sem):
    t = pl.program_id(0)
    n_tiles = pl.num_programs(0)

    @pl.when(jnp.logical_and(t == 0, tv_ref[0] != 0))
    def _():
        _start_row_copies(x_hbm, tok_ref, 0, 1, buf.at[0], sem.at[0], TM_E, 1)

    @pl.when(jnp.logical_and(t + 1 < n_tiles, tv_ref[jnp.minimum(t + 1, n_tiles - 1)] != 0))
    def _():
        nxt = (t + 1) % 2
        _start_row_copies(x_hbm, tok_ref, (t + 1) * TM_E, 1, buf.at[nxt], sem.at[nxt], TM_E, 1)

    @pl.when(tv_ref[t] != 0)
    def _():
        slot = t % 2
        _wait_row_copies(x_hbm, buf.at[slot], sem.at[slot], TM_E)
        o_ref[...] = buf[slot].astype(BF16)

    @pl.when(tv_ref[t] == 0)
    def _():
        o_ref[...] = jnp.zeros(o_ref.shape, BF16)


def _gather_rows(x, row_token, tile_valid):
    d = x.shape[1]
    grid_spec = pltpu.PrefetchScalarGridSpec(
        num_scalar_prefetch=2,
        grid=(N_TILES_E,),
        in_specs=[pl.BlockSpec(memory_space=pl.ANY)],
        out_specs=pl.BlockSpec((TM_E, d), lambda t, tok, tv: (t, 0)),
        scratch_shapes=[pltpu.VMEM((2, TM_E, d), F32), pltpu.SemaphoreType.DMA((2,))])
    return pl.pallas_call(
        _gather_rows_body,
        out_shape=jax.ShapeDtypeStruct((N_ROWS_E, d), BF16),
        grid_spec=grid_spec,
        compiler_params=_params(1),
        name="gather_rows",
    )(row_token, tile_valid, x)


def _combine_body(row_ref, ys_hbm, gate_ref, res_ref, g_ref, b_ref, o_ref, ob_ref, buf, sem):
    t = pl.program_id(0)
    n_tiles = pl.num_programs(0)
    tm = TM_COMB

    def start(tile, slot):
        for k in range(TOP_K):
            _start_row_copies(ys_hbm, row_ref, TOP_K * tile * tm + k, TOP_K, buf.at[slot, k], sem.at[slot], tm,
                              N_DMA_QUEUES)

    @pl.when(t == 0)
    def _():
        start(0, 0)

    @pl.when(t + 1 < n_tiles)
    def _():
        start(t + 1, (t + 1) % 2)

    slot = t % 2
    for k in range(TOP_K):
        _wait_row_copies(ys_hbm, buf.at[slot, k], sem.at[slot], tm)
    y = gate_ref[:, 0:1] * buf[slot, 0] + gate_ref[:, 1:2] * buf[slot, 1]
    _residual_ln_store(y, res_ref, g_ref, b_ref, o_ref, ob_ref)


def _combine(ys, row, gate, res, ln):
    m, n = res.shape
    tm = TM_COMB
    rows = lambda i, r: (i, 0)
    grid_spec = pltpu.PrefetchScalarGridSpec(
        num_scalar_prefetch=1,
        grid=(m // tm,),
        in_specs=[pl.BlockSpec(memory_space=pl.ANY),
                  pl.BlockSpec((tm, TOP_K), rows),
                  pl.BlockSpec((tm, n), rows),
                  _layer_spec((1, n), ln[2]),
                  _layer_spec((1, n), ln[2])],
        out_specs=(pl.BlockSpec((tm, n), rows), pl.BlockSpec((tm, n), rows)),
        scratch_shapes=[pltpu.VMEM((2, TOP_K, tm, n), F32), pltpu.SemaphoreType.DMA((2,))])
    return pl.pallas_call(
        _combine_body,
        out_shape=(jax.ShapeDtypeStruct((m, n), F32), jax.ShapeDtypeStruct((m, n), BF16)),
        grid_spec=grid_spec,
        compiler_params=_params(1),
        name="combine",
    )(row, ys, gate, res, ln[0], ln[1])


def kernel(x_prompt, x_sample, state_a, state_b, state_d, cache_mem_k, cache_mem_v, mem_prompt, w_in, conv_a, pool_w, pool_scale, sg_ln_g, sg_ln_b, sg_w, sg_b, conv_d, cd_ln_g, cd_ln_b, w_out, w_q, w_k, w_v, w_o, ln_g, ln_b, dense_w_gate, dense_w_up, dense_w_down, moe_router, moe_w_gate, moe_w_up, moe_w_down):
    d, g = D_MODEL, D_GRP
    x = jnp.concatenate([x_prompt.reshape(N_PROMPT, d), x_sample.reshape(DEC_BATCH, d)], axis=0)
    xb = x.astype(BF16)
    mem_b = mem_prompt.reshape(BATCH * N_MEM, d).astype(BF16)
    n_tiles = N_TOK // TM
    all_valid = jnp.full((n_tiles,), TILE_PARTS, jnp.int32)
    spare = jnp.zeros((N_TOK, d), BF16)
    moe_wg = moe_w_gate.reshape(-1, d, D_FF)
    moe_wu = moe_w_up.reshape(-1, d, D_FF)
    moe_wd = moe_w_down.reshape(-1, D_FF, d)
    vecs = lambda v: v.reshape(v.shape[0], 1, v.shape[1])
    ps, lg, lb, dg, db = (vecs(v) for v in (pool_scale, sg_ln_g, sg_ln_b, cd_ln_g, cd_ln_b))
    sbt = sg_b.transpose(0, 2, 1)
    w0 = vecs(jnp.repeat(sg_w[:, :, 0, 0], CHUNK, axis=1))
    b0 = vecs(jnp.repeat(sg_b[:, :, 0], CHUNK, axis=1))
    n_norms = ln_g.shape[1]
    ln_all = (ln_g.reshape(DEPTH * n_norms, 1, d), ln_b.reshape(DEPTH * n_norms, 1, d))

    sa_p, sb_p, sd_p, mk_p, mv_p, sa_s, sb_s, sd_s, sc_s = [], [], [], [], [], [], [], [], []
    for l in range(DEPTH):
        ln = lambda i: ln_all + (l * n_norms + i,)
        h = _matmul(xb, w_in, l, F32, TM, TN, "w_in")
        y_p, ha, hb, hd = _mix_prompt(spare, h, l, conv_a, pool_w, ps, lg, lb, sg_w, sbt, conv_d, dg, db)
        y_mix, ta, tb, td, tc = _mix_sample(
            y_p, h, state_a[l].reshape(DEC_BATCH, -1), state_b[l].reshape(DEC_BATCH, -1), state_d[l].reshape(DEC_BATCH, -1),
            l, conv_a, pool_w, ps, lg, lb, w0, b0, conv_d, dg, db)
        x, xb = _mm_ln_resident(y_mix, w_out, l, x, ln(0), "w_out_ln")

        q = _matmul(xb, w_q, l, BF16, TM, TN, "w_q")
        mk = _matmul(mem_b, w_k, l, F32, 512, TN, "w_k")
        mv = _matmul(mem_b, w_v, l, F32, 512, TN, "w_v")
        o_p = _attn_prompt(y_mix, q, mk, mv)
        o_s = _attn_sample(q[N_PROMPT:].astype(F32), cache_mem_k, cache_mem_v, l)
        o = _paste_rows(o_p, o_s.astype(BF16))
        j = l // 2
        is_moe = l % 2 == 1
        x, xb = _mm_ln_resident(o, w_o, l, x, ln(1), "w_o_ln")
        spare = o

        if not is_moe:
            tabs = _run_tables(jnp.full((n_tiles,), j, jnp.int32), all_valid)
            hmid = _ffn_up(xb, dense_w_gate, dense_w_up, tabs, TM)
            y = _ffn_down(hmid, dense_w_down, tabs, TM)
            x, xb = _add_ln(y, x, ln(2))
        else:
            idx, gate = _router(x, moe_router[j])
            row, row_token, tile_expert, tile_valid = _route_tables(idx, j * N_EXPERTS)
            tabs = _run_tables(tile_expert, tile_valid)
            xs = _gather_rows(x, row_token, tile_valid)
            hmid = _ffn_up(xs, moe_wg, moe_wu, tabs, TM_E)
            ys = _ffn_down(hmid, moe_wd, tabs, TM_E)
            x, xb = _combine(ys, row, gate, x, ln(2))

        sa_p.append(ha); sb_p.append(hb); sd_p.append(hd)
        mk_p.append(mk.reshape(BATCH, N_MEM, N_HEADS_X, D_HEAD_X)); mv_p.append(mv.reshape(BATCH, N_MEM, N_HEADS_X, D_HEAD_X))
        sa_s.append(ta.reshape(DEC_BATCH, CONV_A - 1, g)); sb_s.append(tb.reshape(DEC_BATCH, POOL_HIST, g))
        sd_s.append(td.reshape(DEC_BATCH, CONV_D - 1, g)); sc_s.append(tc.reshape(DEC_BATCH, 1, g))

    return (x[:N_PROMPT].reshape(BATCH, SEQ, d), x[N_PROMPT:].reshape(DEC_BATCH, 1, d),
            jnp.stack(sa_p), jnp.stack(sb_p), jnp.stack(sd_p), jnp.stack(mk_p), jnp.stack(mv_p),
            jnp.stack(sa_s), jnp.stack(sb_s), jnp.stack(sd_s), jnp.stack(sc_s))
```
